```python
import math
import jax, jax.numpy as jnp
from jax import lax
import numpy as np

D_MODEL = 1024
BATCH = 2
SEQ = 8192
DEPTH = 4
DEC_BATCH = 32
DEC_SEQ = 1
PAST_LEN = 8192
PAGE_SIZE = 128

N_META = 16
N_MIXERS = 2
N_ATTN = (DEPTH + 1) // 2
N_GLA = DEPTH // 2
A_HEADS = 8
A_HEAD_DIM = 64
A_QK = A_HEADS * 2 * A_HEAD_DIM
A_V = A_HEADS * 2 * A_HEAD_DIM
A_IN = 2 * A_QK + 2 * A_V
Q_BLOCK = 128
G_HEADS = 4
G_DK = D_MODEL // 2 // G_HEADS
G_DV = D_MODEL // G_HEADS
G_QK = G_HEADS * G_DK
G_V = G_HEADS * G_DV
G_RANK = 16
G_NORMALIZER = 16.0
G_CHUNK = 64
G_IN = 2 * G_QK + 2 * G_V + G_RANK
DN_ALPHA = (2.0 * DEPTH) ** 0.25
DN_BETA = (8.0 * DEPTH) ** -0.25
LN_EPS = 1e-5
RMS_EPS = 1e-5

kernel_name = 'hybrid_diffattn_gla_deepnorm_step'


def layer_norm(x, g, b):
    xf = x.astype(jnp.float32)
    mu = jnp.mean(xf, axis=-1, keepdims=True)
    var = jnp.mean(jnp.square(xf - mu), axis=-1, keepdims=True)
    return ((xf - mu) * lax.rsqrt(var + LN_EPS) * g.astype(jnp.float32) + b.astype(jnp.float32)).astype(x.dtype)


def rms_norm(x, w):
    xf = x.astype(jnp.float32)
    return xf * lax.rsqrt(jnp.mean(jnp.square(xf), axis=-1, keepdims=True) + RMS_EPS) * w.astype(jnp.float32)


def alibi_slopes():
    return 2.0 ** (-8.0 * jnp.arange(1, A_HEADS + 1, dtype=jnp.float32) / A_HEADS)


def diff_attend(q, k, v, q_pos, k_pos, lam):
    qf = q.astype(jnp.float32) * (A_HEAD_DIM ** -0.5)
    kf = k.astype(jnp.float32)
    dist = (q_pos[:, None] - k_pos[None, :]).astype(jnp.float32)
    bias = -alibi_slopes()[:, None, None] * dist[None]
    visible = k_pos[None, :] <= q_pos[:, None]

    def probs(qa, ka):
        s = jnp.einsum('bqhd,bkhd->bhqk', qa, ka) + bias
        return jax.nn.softmax(jnp.where(visible, s, -jnp.inf), axis=-1)

    a = probs(qf[..., :A_HEAD_DIM], kf[..., :A_HEAD_DIM]) - lam * probs(qf[..., A_HEAD_DIM:], kf[..., A_HEAD_DIM:])
    return jnp.einsum('bhqk,bkhd->bqhd', a, v.astype(jnp.float32))


def diff_attn_layer(h_p, h_s, k_past, v_past, w_in, lq1, lk1, lq2, lk2, subln_w, w_out, layer_idx):
    lam_init = 0.8 - 0.6 * math.exp(-0.3 * layer_idx)
    lam = (jnp.exp(jnp.sum(lq1.astype(jnp.float32) * lk1.astype(jnp.float32)))
           - jnp.exp(jnp.sum(lq2.astype(jnp.float32) * lk2.astype(jnp.float32))) + lam_init)

    def project(h):
        b, t, _ = h.shape
        z = h @ w_in
        q, k, v, g = jnp.split(z, [A_QK, 2 * A_QK, 2 * A_QK + A_V], axis=-1)
        rs = lambda a: a.reshape(b, t, A_HEADS, 2 * A_HEAD_DIM)
        return rs(q), rs(k), rs(v), g

    def finish(o, g):
        b, t = o.shape[0], o.shape[1]
        o = (rms_norm(o, subln_w) * (1.0 - lam_init)).reshape(b, t, A_V).astype(g.dtype)
        return (o * jax.nn.silu(g)) @ w_out

    qp, kp, vp, gp = project(h_p)
    bsz, L = h_p.shape[0], h_p.shape[1]
    n_blk = -(-L // Q_BLOCK)
    Lp = n_blk * Q_BLOCK
    pad = ((0, 0), (0, Lp - L), (0, 0), (0, 0))
    qpp, kpp, vpp = jnp.pad(qp, pad), jnp.pad(kp, pad), jnp.pad(vp, pad)
    k_pos = jnp.arange(Lp, dtype=jnp.int32)
    q_blocks = qpp.reshape(bsz, n_blk, Q_BLOCK, A_HEADS, 2 * A_HEAD_DIM).transpose(1, 0, 2, 3, 4)

    def block(args):
        qb, start = args
        q_pos = start + jnp.arange(Q_BLOCK, dtype=jnp.int32)
        return diff_attend(qb, kpp, vpp, q_pos, k_pos, lam)

    o_blocks = lax.map(block, (q_blocks, jnp.arange(n_blk, dtype=jnp.int32) * Q_BLOCK))
    o_p = o_blocks.transpose(1, 0, 2, 3, 4).reshape(bsz, Lp, A_HEADS, 2 * A_HEAD_DIM)[:, :L]
    out_p = finish(o_p, gp)

    qs, ks, vs, gs = project(h_s)
    P, T = k_past.shape[1], h_s.shape[1]
    k_all = jnp.concatenate([k_past.astype(ks.dtype), ks], axis=1)
    v_all = jnp.concatenate([v_past.astype(vs.dtype), vs], axis=1)
    q_pos_s = P + jnp.arange(T, dtype=jnp.int32)
    k_pos_s = jnp.arange(P + T, dtype=jnp.int32)
    o_s = diff_attend(qs, k_all, v_all, q_pos_s, k_pos_s, lam)
    out_s = finish(o_s, gs)
    return out_p, out_s, kp, vp, ks, vs


def gla_chunk(q, k, v, log_a, S0):
    C = q.shape[1]
    b = jnp.cumsum(log_a, axis=1)
    o_inter = jnp.einsum('bthk,bhkv->bthv', q * jnp.exp(b), S0)
    causal = jnp.tril(jnp.ones((C, C), dtype=bool))[None, :, :, None, None]
    decay = jnp.exp(jnp.where(causal, b[:, :, None] - b[:, None, :], -jnp.inf))
    A = jnp.einsum('bthk,bshk,btshk->bhts', q, k, decay)
    o = o_inter + jnp.einsum('bhts,bshv->bthv', A, v)
    b_last = b[:, -1]
    S1 = jnp.exp(b_last)[..., None] * S0 + jnp.einsum('bshk,bshv->bhkv', k * jnp.exp(b_last[:, None] - b), v)
    return o, S1


def gla_layer(h_p, h_s, S_past, w_in, w_gate_up, b_gate, norm_w, w_out):
    def project(h):
        bsz, t, _ = h.shape
        z = h @ w_in
        q, k, v, g, r = jnp.split(z, [G_QK, 2 * G_QK, 2 * G_QK + G_V, 2 * G_QK + 2 * G_V], axis=-1)
        log_a = jax.nn.log_sigmoid((r @ w_gate_up + b_gate).astype(jnp.float32)) / G_NORMALIZER
        hk = lambda a: a.astype(jnp.float32).reshape(bsz, t, G_HEADS, G_DK)
        vv = v.astype(jnp.float32).reshape(bsz, t, G_HEADS, G_DV)
        return hk(q) * (G_DK ** -0.5), hk(k), vv, hk(log_a), g

    def finish(o, g):
        bsz, t = o.shape[0], o.shape[1]
        o = rms_norm(o, norm_w).reshape(bsz, t, G_V).astype(g.dtype)
        return (o * jax.nn.silu(g)) @ w_out

    q, k, v, la, g = project(h_p)
    bsz = h_p.shape[0]
    S0 = jnp.zeros((bsz, G_HEADS, G_DK, G_DV), jnp.float32)
    o_meta, S = gla_chunk(q[:, :N_META], k[:, :N_META], v[:, :N_META], la[:, :N_META], S0)

    def to_chunks(a):
        real = a[:, N_META:]
        n = real.shape[1] // G_CHUNK
        return real.reshape(bsz, n, G_CHUNK, *a.shape[2:]).swapaxes(0, 1)

    def step(S_c, xs):
        qc, kc, vc, lc = xs
        o_c, S_n = gla_chunk(qc, kc, vc, lc, S_c)
        return S_n, o_c

    S_p, o_chunks = lax.scan(step, S, (to_chunks(q), to_chunks(k), to_chunks(v), to_chunks(la)))
    o_real = o_chunks.swapaxes(0, 1).reshape(bsz, -1, G_HEADS, G_DV)
    out_p = finish(jnp.concatenate([o_meta, o_real], axis=1), g)

    qs, ks, vs, las, gs = project(h_s)
    o_s, S_s = gla_chunk(qs, ks, vs, las, S_past.astype(jnp.float32))
    out_s = finish(o_s, gs)
    return out_p, out_s, S_p, S_s


def setup_inputs(seed: int = 0) -> dict:
    key = jax.random.key(seed)
    ks = jax.random.split(key, 24)
    n_pages = PAST_LEN // PAGE_SIZE
    n_phys = (5 * DEC_BATCH * n_pages + 3) // 4
    nrm = lambda k_, shape: jax.random.normal(k_, shape, jnp.float32)
    page_table = jax.random.permutation(ks[5], n_phys)[:DEC_BATCH * n_pages].reshape(DEC_BATCH, n_pages).astype(jnp.int32)
    return {
        'x_prompt': nrm(ks[0], (BATCH, SEQ, D_MODEL)),
        'x_sample': nrm(ks[1], (DEC_BATCH, DEC_SEQ, D_MODEL)),
        'cache_k': nrm(ks[2], (n_phys, N_ATTN, PAGE_SIZE, A_HEADS, 2 * A_HEAD_DIM)),
        'cache_v': nrm(ks[3], (n_phys, N_ATTN, PAGE_SIZE, A_HEADS, 2 * A_HEAD_DIM)),
        'state_gla': 0.5 * nrm(ks[4], (DEC_BATCH, N_GLA, G_HEADS, G_DK, G_DV)),
        'page_table': page_table,
        'meta_tokens': nrm(ks[6], (N_META, D_MODEL)),
        'attn_w_in': nrm(ks[7], (N_ATTN, D_MODEL, A_IN)) * D_MODEL ** -0.5,
        'attn_lq1': 0.1 * nrm(ks[8], (N_ATTN, A_HEAD_DIM)),
        'attn_lk1': 0.1 * nrm(ks[9], (N_ATTN, A_HEAD_DIM)),
        'attn_lq2': 0.1 * nrm(ks[10], (N_ATTN, A_HEAD_DIM)),
        'attn_lk2': 0.1 * nrm(ks[11], (N_ATTN, A_HEAD_DIM)),
        'attn_subln_w': 1.0 + 0.01 * nrm(ks[12], (N_ATTN, 2 * A_HEAD_DIM)),
        'attn_w_out': nrm(ks[13], (N_ATTN, A_V, D_MODEL)) * (A_V ** -0.5 * DN_BETA),
        'gla_w_in': nrm(ks[14], (N_GLA, D_MODEL, G_IN)) * D_MODEL ** -0.5,
        'gla_w_gate_up': nrm(ks[15], (N_GLA, G_RANK, G_QK)) * G_RANK ** -0.5,
        'gla_b_gate': 0.01 * nrm(ks[16], (N_GLA, G_QK)),
        'gla_norm_w': 1.0 + 0.01 * nrm(ks[17], (N_GLA, G_DV)),
        'gla_w_out': nrm(ks[18], (N_GLA, G_V, D_MODEL)) * (G_V ** -0.5 * DN_BETA),
        'ln_g': 1.0 + 0.01 * nrm(ks[19], (DEPTH, D_MODEL)),
        'ln_b': 0.01 * nrm(ks[20], (DEPTH, D_MODEL)),
    }


def reference(x_prompt, x_sample, cache_k, cache_v, state_gla, page_table, meta_tokens,
              attn_w_in, attn_lq1, attn_lk1, attn_lq2, attn_lk2, attn_subln_w, attn_w_out,
              gla_w_in, gla_w_gate_up, gla_b_gate, gla_norm_w, gla_w_out, ln_g, ln_b):
    bsz = x_prompt.shape[0]
    dbsz = x_sample.shape[0]
    n_pages = page_table.shape[1]
    meta = jnp.broadcast_to(meta_tokens[None].astype(x_prompt.dtype), (bsz, N_META, D_MODEL))
    h_p = jnp.concatenate([meta, x_prompt], axis=1)
    h_s = x_sample
    k_p_rows, v_p_rows, k_s_rows, v_s_rows, S_p_list, S_s_list = [], [], [], [], [], []
    for i in range(DEPTH):
        if i % N_MIXERS == 0:
            a = i // N_MIXERS
            k_past = cache_k[page_table, a].reshape(dbsz, n_pages * PAGE_SIZE, A_HEADS, 2 * A_HEAD_DIM)
            v_past = cache_v[page_table, a].reshape(dbsz, n_pages * PAGE_SIZE, A_HEADS, 2 * A_HEAD_DIM)
            out_p, out_s, kp, vp, kn, vn = diff_attn_layer(
                h_p, h_s, k_past, v_past, attn_w_in[a], attn_lq1[a], attn_lk1[a], attn_lq2[a], attn_lk2[a],
                attn_subln_w[a], attn_w_out[a], i)
            k_p_rows.append(kp)
            v_p_rows.append(vp)
            k_s_rows.append(kn)
            v_s_rows.append(vn)
        else:
            gi = i // N_MIXERS
            out_p, out_s, S_p, S_s = gla_layer(h_p, h_s, state_gla[:, gi], gla_w_in[gi], gla_w_gate_up[gi],
                                               gla_b_gate[gi], gla_norm_w[gi], gla_w_out[gi])
            S_p_list.append(S_p)
            S_s_list.append(S_s)
        h_p = layer_norm(DN_ALPHA * h_p + out_p, ln_g[i], ln_b[i])
        h_s = layer_norm(DN_ALPHA * h_s + out_s, ln_g[i], ln_b[i])
    y_prompt = h_p[:, N_META:]
    return (y_prompt, h_s,
            jnp.stack(k_p_rows, axis=1), jnp.stack(v_p_rows, axis=1), jnp.stack(S_p_list, axis=1),
            jnp.stack(k_s_rows, axis=1), jnp.stack(v_s_rows, axis=1), jnp.stack(S_s_list, axis=1))
```

```python
import functools
import math

import numpy as np
import jax
import jax.numpy as jnp
from jax import lax
from jax.experimental import pallas as pl
from jax.experimental.pallas import tpu as pltpu

F32 = jnp.float32
BF16 = jnp.bfloat16

D_MODEL = 1024
DEPTH = 4
N_META = 16
PAGE_SIZE = 128
A_HEADS = 8
A_HEAD_DIM = 64
A_HD2 = 2 * A_HEAD_DIM
A_QK = A_HEADS * A_HD2
G_HEADS = 4
G_DK = 128
G_DV = 256
G_QK = G_HEADS * G_DK
G_V = G_HEADS * G_DV
G_RANK = 16
G_RANK_PAD = 128
G_NORMALIZER = 16.0
G_CHUNK = 64
G_SUB = 16
DN_ALPHA = (2.0 * DEPTH) ** 0.25
LN_EPS = 1e-5
RMS_EPS = 1e-5

MASK_VALUE = -1e30
ATT_TILE = 256
ATT_WIDE = 512
GLA_ROWS = 256
PROJ_ROWS = 256
DEC_PAGES = 4
VMEM_LIMIT = 56 * 1024 * 1024

NT_DIMS = (((1,), (1,)), ((), ()))
TN_DIMS = (((0,), (0,)), ((), ()))


def _cparams(sem):
    return pltpu.CompilerParams(dimension_semantics=sem, vmem_limit_bytes=VMEM_LIMIT)


def _attn_proj_kernel(x_ref, w_ref, q_ref, k_ref, v_ref, kb_ref, vb_ref, g_ref):
    x = x_ref[...].astype(BF16)

    def cols(c):
        return jnp.dot(x, w_ref[:, c * A_QK:(c + 1) * A_QK], preferred_element_type=F32)

    q_ref[...] = (cols(0) * (A_HEAD_DIM ** -0.5)).astype(BF16)
    k = cols(1)
    k_ref[...] = k
    kb_ref[...] = k.astype(BF16)
    v = cols(2)
    v_ref[...] = v
    vb_ref[...] = v.astype(BF16)
    g_ref[...] = cols(3)


def attn_proj(x, w_bf, tm):
    m = x.shape[0]
    row = lambda i: (i, 0)
    blk = pl.BlockSpec((tm, A_QK), row)
    sds = lambda dt: jax.ShapeDtypeStruct((m, A_QK), dt)
    return pl.pallas_call(
        _attn_proj_kernel,
        grid=(m // tm,),
        in_specs=[pl.BlockSpec((tm, D_MODEL), row),
                  pl.BlockSpec((D_MODEL, 4 * A_QK), lambda i: (0, 0))],
        out_specs=[blk] * 6,
        out_shape=[sds(BF16), sds(F32), sds(F32), sds(BF16), sds(BF16), sds(F32)],
        compiler_params=_cparams(("parallel",)),
        name="attn_proj",
    )(x, w_bf)


def _log_sigmoid(x):
    return jnp.minimum(x, 0.0) - jnp.log1p(jnp.exp(-jnp.abs(x)))


def _gla_proj_kernel(x_ref, w_ref, wup_ref, bg_ref, q_ref, k_ref, v_ref, g_ref, la_ref):
    x = x_ref[...].astype(BF16)

    def cols(a, b):
        return jnp.dot(x, w_ref[:, a:b], preferred_element_type=F32)

    q_ref[...] = cols(0, G_QK) * (G_DK ** -0.5)
    k_ref[...] = cols(G_QK, 2 * G_QK)
    v_ref[...] = cols(2 * G_QK, 2 * G_QK + G_V)
    g_ref[...] = cols(2 * G_QK + G_V, 2 * G_QK + 2 * G_V)
    r = cols(2 * G_QK + 2 * G_V, 2 * G_QK + 2 * G_V + G_RANK_PAD)
    gate = jnp.dot(r.astype(BF16), wup_ref[...], preferred_element_type=F32) + bg_ref[...]
    la_ref[...] = _log_sigmoid(gate) / G_NORMALIZER


def gla_proj(x, w_bf, wup_bf, b_gate, tm):
    m = x.shape[0]
    n_in = w_bf.shape[1]
    row = lambda i: (i, 0)
    full = lambda i: (0, 0)
    return pl.pallas_call(
        _gla_proj_kernel,
        grid=(m // tm,),
        in_specs=[pl.BlockSpec((tm, D_MODEL), row),
                  pl.BlockSpec((D_MODEL, n_in), full),
                  pl.BlockSpec((G_RANK_PAD, G_QK), full),
                  pl.BlockSpec((1, G_QK), full)],
        out_specs=[pl.BlockSpec((tm, G_QK), row), pl.BlockSpec((tm, G_QK), row),
                   pl.BlockSpec((tm, G_V), row), pl.BlockSpec((tm, G_V), row),
                   pl.BlockSpec((tm, G_QK), row)],
        out_shape=[jax.ShapeDtypeStruct((m, G_QK), F32), jax.ShapeDtypeStruct((m, G_QK), F32),
                   jax.ShapeDtypeStruct((m, G_V), F32), jax.ShapeDtypeStruct((m, G_V), F32),
                   jax.ShapeDtypeStruct((m, G_QK), F32)],
        compiler_params=_cparams(("parallel",)),
        name="gla_proj",
    )(x, w_bf, wup_bf, b_gate)


def _finish_kernel(o_ref, g_ref, h_ref, w_ref, nw_ref, lg_ref, lb_ref, out_ref, *, head_dim, scale):
    o = o_ref[...]
    nw = nw_ref[...]
    parts = []
    for c in range(o.shape[1] // head_dim):
        oh = o[:, c * head_dim:(c + 1) * head_dim]
        ms = jnp.mean(oh * oh, axis=-1, keepdims=True)
        parts.append(oh * lax.rsqrt(ms + RMS_EPS) * nw)
    on = jnp.concatenate(parts, axis=1)
    if scale != 1.0:
        on = on * scale
    g = g_ref[...]
    y = on * (g * jax.nn.sigmoid(g))
    y = jnp.dot(y.astype(BF16), w_ref[...], preferred_element_type=F32)
    x = DN_ALPHA * h_ref[...] + y
    mu = jnp.mean(x, axis=-1, keepdims=True)
    xc = x - mu
    var = jnp.mean(xc * xc, axis=-1, keepdims=True)
    out_ref[...] = xc * lax.rsqrt(var + LN_EPS) * lg_ref[...] + lb_ref[...]


def finish(o, g, h, w_bf, norm_w, ln_g, ln_b, head_dim, scale, tm):
    m, n = o.shape
    row = lambda i: (i, 0)
    full = lambda i: (0, 0)
    return pl.pallas_call(
        functools.partial(_finish_kernel, head_dim=head_dim, scale=scale),
        grid=(m // tm,),
        in_specs=[pl.BlockSpec((tm, n), row), pl.BlockSpec((tm, n), row),
                  pl.BlockSpec((tm, D_MODEL), row),
                  pl.BlockSpec((n, D_MODEL), full),
                  pl.BlockSpec((1, head_dim), full),
                  pl.BlockSpec((1, D_MODEL), full), pl.BlockSpec((1, D_MODEL), full)],
        out_specs=pl.BlockSpec((tm, D_MODEL), row),
        out_shape=jax.ShapeDtypeStruct((m, D_MODEL), F32),
        compiler_params=_cparams(("parallel",)),
        name="finish",
    )(o, g, h, w_bf, norm_w.reshape(1, head_dim), ln_g.reshape(1, D_MODEL), ln_b.reshape(1, D_MODEL))


def _lambda_value(lam_ref, lam_init):
    lv = lam_ref[...]
    e1 = jnp.exp(jnp.sum(lv[0:1] * lv[1:2], axis=-1, keepdims=True))
    e2 = jnp.exp(jnp.sum(lv[2:3] * lv[3:4], axis=-1, keepdims=True))
    return e1 - e2 + lam_init


def _split_maps(q):
    lane = lax.broadcasted_iota(jnp.int32, q.shape, 1)
    zero = jnp.zeros_like(q)
    return jnp.concatenate([jnp.where(lane < A_HEAD_DIM, q, zero),
                            jnp.where(lane >= A_HEAD_DIM, q, zero)], axis=0)


def _flash_kernel(slope_ref, lam_ref, q_ref, k_ref, v_ref, bias_ref, o_ref,
                  m_sc, l_sc, acc_sc, *, lam_init):
    t = ATT_TILE
    h = pl.program_id(1)
    qi = pl.program_id(2)
    slope = slope_ref[h]
    qq = _split_maps(q_ref[...])
    bias = bias_ref[...]
    bias2 = jnp.concatenate([bias, bias], axis=0)

    def scores(start, width):
        k = k_ref[pl.ds(start, width), :]
        return lax.dot_general(qq, k, NT_DIMS, preferred_element_type=F32)

    def accumulate(s, start, width, offset):
        m_old = m_sc[...]
        m_rel = jnp.maximum(m_old - offset, jnp.max(s, axis=-1, keepdims=True))
        m_new = m_rel + offset
        alpha = jnp.exp(m_old - m_new)
        p = jnp.exp(s - m_rel)
        l_sc[...] = alpha * l_sc[...] + jnp.sum(p, axis=-1, keepdims=True)
        v = v_ref[pl.ds(start, width), :]
        acc_sc[...] = alpha * acc_sc[...] + jnp.dot(p.astype(BF16), v, preferred_element_type=F32)
        m_sc[...] = m_new

    d0 = pl.multiple_of(qi * t, t)
    s = scores(d0, t) + bias2
    row = lax.broadcasted_iota(jnp.int32, (t, t), 0)
    col = lax.broadcasted_iota(jnp.int32, (t, t), 1)
    keep = col <= row
    s = jnp.where(jnp.concatenate([keep, keep], axis=0), s, MASK_VALUE)
    m0 = jnp.max(s, axis=-1, keepdims=True)
    p = jnp.exp(s - m0)
    m_sc[...] = m0
    l_sc[...] = jnp.sum(p, axis=-1, keepdims=True)
    acc_sc[...] = jnp.dot(p.astype(BF16), v_ref[pl.ds(d0, t), :], preferred_element_type=F32)

    @pl.when(qi % 2 == 1)
    def _():
        start = pl.multiple_of((qi - 1) * t, t)
        accumulate(scores(start, t) + bias2, start, t, -slope * t)

    wide_bias = jnp.concatenate([bias2, bias2 + slope * t], axis=1)

    def body(j, carry):
        start = pl.multiple_of(j * ATT_WIDE, ATT_WIDE)
        offset = slope * (start - qi * t).astype(F32)
        accumulate(scores(start, ATT_WIDE) + wide_bias, start, ATT_WIDE, offset)
        return carry

    lax.fori_loop(0, qi // 2, body, 0)

    lam = _lambda_value(lam_ref, lam_init)
    o = acc_sc[...] / l_sc[...]
    o_ref[...] = o[:t] - lam * o[t:]


def flash_prompt(q, kb, vb, lam_rows, lam_init, batch, lp):
    t = ATT_TILE
    nq = lp // t
    slopes = np.array([2.0 ** (-8.0 * (i + 1) / A_HEADS) for i in range(A_HEADS)], np.float32)
    rel = (np.arange(t)[None, :] - np.arange(t)[:, None]).astype(np.float32)
    bias = slopes[:, None, None] * rel[None]
    return pl.pallas_call(
        functools.partial(_flash_kernel, lam_init=lam_init),
        grid=(batch, A_HEADS, nq),
        in_specs=[pl.BlockSpec(memory_space=pltpu.SMEM),
                  pl.BlockSpec((4, A_HEAD_DIM), lambda b, h, i: (0, 0)),
                  pl.BlockSpec((t, A_HD2), lambda b, h, i: (b * nq + i, h)),
                  pl.BlockSpec((lp, A_HD2), lambda b, h, i: (b, h)),
                  pl.BlockSpec((lp, A_HD2), lambda b, h, i: (b, h)),
                  pl.BlockSpec((None, t, t), lambda b, h, i: (h, 0, 0))],
        out_specs=pl.BlockSpec((t, A_HD2), lambda b, h, i: (b * nq + i, h)),
        out_shape=jax.ShapeDtypeStruct((batch * lp, A_QK), F32),
        scratch_shapes=[pltpu.VMEM((2 * t, 1), F32), pltpu.VMEM((2 * t, 1), F32),
                        pltpu.VMEM((2 * t, A_HD2), F32)],
        compiler_params=_cparams(("parallel", "parallel", "arbitrary")),
        name="flash_prompt",
    )(jnp.asarray(slopes), lam_rows, q, kb, vb, jnp.asarray(bias))


def _decode_kernel(pt_ref, lam_ref, q_ref, kn_ref, vn_ref, bias_ref, slope_ref, *rest, lam_init, past_len):
    del pt_ref
    k_refs = rest[:DEC_PAGES]
    v_refs = rest[DEC_PAGES:2 * DEC_PAGES]
    o_ref, m_sc, l_sc, acc_sc = rest[2 * DEC_PAGES:]
    j = pl.program_id(1)
    qq = _split_maps(q_ref[...]).astype(BF16)
    slope_col = slope_ref[...]

    @pl.when(j == 0)
    def _():
        kn = kn_ref[...]
        kn2 = jnp.concatenate([kn, kn], axis=0)
        m_sc[...] = jnp.sum(qq.astype(F32) * kn2, axis=-1, keepdims=True)
        l_sc[...] = jnp.ones_like(l_sc)
        vn = vn_ref[...]
        acc_sc[...] = jnp.concatenate([vn, vn], axis=0)

    bias = bias_ref[...]
    for i in range(DEC_PAGES):
        page = j * DEC_PAGES + i
        k = k_refs[i][...].reshape(PAGE_SIZE * A_HEADS, A_HD2).astype(BF16)
        v = v_refs[i][...].reshape(PAGE_SIZE * A_HEADS, A_HD2).astype(BF16)
        offset = slope_col * (page * PAGE_SIZE - past_len).astype(F32)
        s = lax.dot_general(qq, k, NT_DIMS, preferred_element_type=F32) + bias + offset
        m_old = m_sc[...]
        m_new = jnp.maximum(m_old, jnp.max(s, axis=-1, keepdims=True))
        alpha = jnp.exp(m_old - m_new)
        p = jnp.exp(s - m_new)
        l_sc[...] = alpha * l_sc[...] + jnp.sum(p, axis=-1, keepdims=True)
        acc_sc[...] = alpha * acc_sc[...] + jnp.dot(p.astype(BF16), v, preferred_element_type=F32)
        m_sc[...] = m_new

    @pl.when(j == pl.num_programs(1) - 1)
    def _():
        lam = _lambda_value(lam_ref, lam_init)
        o = acc_sc[...] / l_sc[...]
        o_ref[...] = o[:A_HEADS] - lam * o[A_HEADS:]


def decode_attn(q_s, k_s, v_s, cache_k, cache_v, page_table, layer, lam_rows, lam_init):
    db = q_s.shape[0]
    n_pages = page_table.shape[1]
    past_len = n_pages * PAGE_SIZE
    slopes = np.array([2.0 ** (-8.0 * (i + 1) / A_HEADS) for i in range(A_HEADS)], np.float32)
    rows_h = np.tile(np.arange(A_HEADS), 2)
    cols_t = np.repeat(np.arange(PAGE_SIZE), A_HEADS)
    cols_h = np.tile(np.arange(A_HEADS), PAGE_SIZE)
    bias = np.where(rows_h[:, None] == cols_h[None, :],
                    slopes[rows_h][:, None] * cols_t[None, :].astype(np.float32),
                    np.float32(MASK_VALUE)).astype(np.float32)
    slope_col = slopes[rows_h][:, None]

    hd = lambda a: a.reshape(db, A_HEADS, A_HD2)
    per_b = pl.BlockSpec((None, A_HEADS, A_HD2), lambda b, j, pt: (b, 0, 0))
    const2 = lambda b, j, pt: (0, 0)

    def page_spec(i):
        return pl.BlockSpec((None, None, PAGE_SIZE, A_HEADS, A_HD2),
                            lambda b, j, pt: (pt[b, j * DEC_PAGES + i], layer, 0, 0, 0))

    grid_spec = pltpu.PrefetchScalarGridSpec(
        num_scalar_prefetch=1,
        grid=(db, n_pages // DEC_PAGES),
        in_specs=[pl.BlockSpec((4, A_HEAD_DIM), const2), per_b, per_b, per_b,
                  pl.BlockSpec((2 * A_HEADS, PAGE_SIZE * A_HEADS), const2),
                  pl.BlockSpec((2 * A_HEADS, 1), const2)]
                 + [page_spec(i) for i in range(DEC_PAGES)] * 2,
        out_specs=per_b,
        scratch_shapes=[pltpu.VMEM((2 * A_HEADS, 1), F32), pltpu.VMEM((2 * A_HEADS, 1), F32),
                        pltpu.VMEM((2 * A_HEADS, A_HD2), F32)],
    )
    out = pl.pallas_call(
        functools.partial(_decode_kernel, lam_init=lam_init, past_len=past_len),
        grid_spec=grid_spec,
        out_shape=jax.ShapeDtypeStruct((db, A_HEADS, A_HD2), F32),
        compiler_params=_cparams(("parallel", "arbitrary")),
        name="decode_attn",
    )(page_table, lam_rows, hd(q_s.astype(F32)), hd(k_s), hd(v_s), jnp.asarray(bias), jnp.asarray(slope_col),
      *([cache_k] * DEC_PAGES), *([cache_v] * DEC_PAGES))
    return out.reshape(db, A_QK)


def _gla_chunk(q, k, v, la, st):
    c = q.shape[0]
    r_i = lax.broadcasted_iota(jnp.int32, (c, c), 0)
    c_i = lax.broadcasted_iota(jnp.int32, (c, c), 1)
    tril = (r_i >= c_i).astype(F32)
    b = jnp.dot(tril, la, precision=lax.Precision.HIGHEST, preferred_element_type=F32)
    vb = v.astype(BF16)

    o_inter = lax.dot_general((q * jnp.exp(b)).astype(BF16), st.astype(BF16), NT_DIMS,
                              preferred_element_type=F32)

    row_c = lax.broadcasted_iota(jnp.int32, (c, 1), 0)
    lane_c = lax.broadcasted_iota(jnp.int32, (G_SUB, c), 1)
    sub_c = lax.broadcasted_iota(jnp.int32, (G_SUB, c), 0)
    o_parts = []
    for blk in range(c // G_SUB):
        r0 = blk * G_SUB
        b_i = b[r0:r0 + G_SUB]
        q_i = q[r0:r0 + G_SUB]
        k_i = k[r0:r0 + G_SUB]
        a_blk = jnp.zeros((G_SUB, c), F32)
        if blk > 0:
            ref = b[r0:r0 + 1]
            q_t = q_i * jnp.exp(b_i - ref)
            k_t = jnp.where(row_c < r0, k * jnp.exp(jnp.minimum(ref - b, 0.0)), 0.0)
            a_blk = lax.dot_general(q_t.astype(BF16), k_t.astype(BF16), NT_DIMS,
                                    preferred_element_type=F32)
        a_diag = jnp.zeros((G_SUB, c), F32)
        for s in range(G_SUB):
            w = jnp.exp(jnp.minimum(b_i - b_i[s:s + 1], 0.0)) * q_i * k_i[s:s + 1]
            a_diag = jnp.where(lane_c == r0 + s, jnp.sum(w, axis=-1, keepdims=True), a_diag)
        in_diag = (lane_c >= r0) & (lane_c - r0 <= sub_c)
        a_blk = a_blk + jnp.where(in_diag, a_diag, 0.0)
        o_parts.append(jnp.dot(a_blk.astype(BF16), vb, preferred_element_type=F32))
    o = o_inter + jnp.concatenate(o_parts, axis=0)

    b_last = b[c - 1:c]
    k_dec = k * jnp.exp(b_last - b)
    st_new = st * jnp.exp(b_last) + lax.dot_general(vb, k_dec.astype(BF16), TN_DIMS,
                                                    preferred_element_type=F32)
    return o, st_new


def _gla_kernel(q_ref, k_ref, v_ref, la_ref, o_ref, s_ref, st_sc, *, valid_len):
    i = pl.program_id(2)

    @pl.when(i == 0)
    def _():
        st_sc[...] = jnp.zeros_like(st_sc)

    st = st_sc[...]
    for c in range(GLA_ROWS // G_CHUNK):
        sl = slice(c * G_CHUNK, (c + 1) * G_CHUNK)
        pos = i * GLA_ROWS + c * G_CHUNK + lax.broadcasted_iota(jnp.int32, (G_CHUNK, 1), 0)
        valid = pos < valid_len
        la = jnp.where(valid, la_ref[sl, :], 0.0)
        k = jnp.where(valid, k_ref[sl, :], 0.0)
        o, st = _gla_chunk(q_ref[sl, :], k, v_ref[sl, :], la, st)
        o_ref[sl, :] = o
    st_sc[...] = st

    @pl.when(i == pl.num_programs(2) - 1)
    def _():
        s_ref[...] = st.T


def gla_prompt(q, k, v, la, batch, lp, valid_len):
    nblk = lp // GLA_ROWS
    qk_spec = pl.BlockSpec((GLA_ROWS, G_DK), lambda b, h, i: (b * nblk + i, h))
    v_spec = pl.BlockSpec((GLA_ROWS, G_DV), lambda b, h, i: (b * nblk + i, h))
    return pl.pallas_call(
        functools.partial(_gla_kernel, valid_len=valid_len),
        grid=(batch, G_HEADS, nblk),
        in_specs=[qk_spec, qk_spec, v_spec, qk_spec],
        out_specs=[v_spec, pl.BlockSpec((None, None, G_DK, G_DV), lambda b, h, i: (b, h, 0, 0))],
        out_shape=[jax.ShapeDtypeStruct((batch * lp, G_V), F32),
                   jax.ShapeDtypeStruct((batch, G_HEADS, G_DK, G_DV), F32)],
        scratch_shapes=[pltpu.VMEM((G_DV, G_DK), F32)],
        compiler_params=_cparams(("parallel", "parallel", "arbitrary")),
        name="gla_prompt",
    )(q, k, v, la)


def _stack_rows(rows, n_rows=16):
    n = rows[0].shape[1]
    idx = lax.broadcasted_iota(jnp.int32, (n_rows, n), 0)
    out = jnp.zeros((n_rows, n), F32)
    for r, x in enumerate(rows):
        out = jnp.where(idx == r, x, out)
    return out.astype(BF16)


def _gla_step_kernel(q_ref, k_ref, v_ref, la_ref, s_ref, o_ref, so_ref):
    q = q_ref[...]
    k = k_ref[...]
    v = v_ref[...]
    a = jnp.exp(la_ref[...])
    ones = jnp.ones((16, G_DV), BF16)
    o_parts = []
    for h in range(G_HEADS):
        ks = slice(h * G_DK, (h + 1) * G_DK)
        vs = slice(h * G_DV, (h + 1) * G_DV)
        a_h = a[:, ks]
        a1 = a_h.astype(BF16)
        r1 = a_h - a1.astype(F32)
        a2 = r1.astype(BF16)
        a3 = (r1 - a2.astype(F32)).astype(BF16)
        a_rows = _stack_rows([a1.astype(F32), a2.astype(F32), a3.astype(F32)])
        a_col = lax.dot_general(a_rows, ones, TN_DIMS, preferred_element_type=F32)
        kv = lax.dot_general(_stack_rows([k[:, ks]]), _stack_rows([v[:, vs]]), TN_DIMS,
                             preferred_element_type=F32)
        s1 = a_col * s_ref[h] + kv
        so_ref[h] = s1
        o = jnp.dot(_stack_rows([q[:, ks]]), s1.astype(BF16), preferred_element_type=F32)
        o_parts.append(o[0:1])
    o_ref[...] = jnp.concatenate(o_parts, axis=1)


def gla_step(q_s, k_s, v_s, la_s, state_gla, layer):
    db = q_s.shape[0]
    r3 = lambda a: a.reshape(db, 1, a.shape[1])
    vec = lambda n: pl.BlockSpec((None, 1, n), lambda b: (b, 0, 0))
    o, s_new = pl.pallas_call(
        _gla_step_kernel,
        grid=(db,),
        in_specs=[vec(G_QK), vec(G_QK), vec(G_V), vec(G_QK),
                  pl.BlockSpec((None, None, G_HEADS, G_DK, G_DV), lambda b: (b, layer, 0, 0, 0))],
        out_specs=[vec(G_V), pl.BlockSpec((None, G_HEADS, G_DK, G_DV), lambda b: (b, 0, 0, 0))],
        out_shape=[jax.ShapeDtypeStruct((db, 1, G_V), F32),
                   jax.ShapeDtypeStruct((db, G_HEADS, G_DK, G_DV), F32)],
        compiler_params=_cparams(("parallel",)),
        name="gla_step",
    )(r3(q_s), r3(k_s), r3(v_s), r3(la_s), state_gla)
    return o.reshape(db, G_V), s_new


def kernel(x_prompt, x_sample, cache_k, cache_v, state_gla, page_table, meta_tokens, attn_w_in, attn_lq1, attn_lk1, attn_lq2, attn_lk2, attn_subln_w, attn_w_out, gla_w_in, gla_w_gate_up, gla_b_gate, gla_norm_w, gla_w_out, ln_g, ln_b):
    batch, seq, _ = x_prompt.shape
    db = x_sample.shape[0]
    length = N_META + seq
    lp = -(-length // ATT_TILE) * ATT_TILE
    meta = jnp.broadcast_to(meta_tokens[None].astype(x_prompt.dtype), (batch, N_META, D_MODEL))
    pad = jnp.zeros((batch, lp - length, D_MODEL), x_prompt.dtype)
    h_p = jnp.concatenate([meta, x_prompt, pad], axis=1).reshape(batch * lp, D_MODEL)
    h_s = x_sample.reshape(db, D_MODEL)

    k_p_rows, v_p_rows, k_s_rows, v_s_rows, s_p_list, s_s_list = [], [], [], [], [], []
    for i in range(DEPTH):
        if i % 2 == 0:
            a = i // 2
            lam_init = 0.8 - 0.6 * math.exp(-0.3 * i)
            w_in = attn_w_in[a].astype(BF16)
            w_out = attn_w_out[a].astype(BF16)
            lam_rows = jnp.stack([attn_lq1[a], attn_lk1[a], attn_lq2[a], attn_lk2[a]]).astype(F32)
            q_p, k_p, v_p, kb_p, vb_p, g_p = attn_proj(h_p, w_in, PROJ_ROWS)
            q_s, k_s, v_s, _, _, g_s = attn_proj(h_s, w_in, db)
            o_p = flash_prompt(q_p, kb_p, vb_p, lam_rows, lam_init, batch, lp)
            o_s = decode_attn(q_s, k_s, v_s, cache_k, cache_v, page_table, a, lam_rows, lam_init)
            fin = functools.partial(finish, w_bf=w_out, norm_w=attn_subln_w[a], ln_g=ln_g[i], ln_b=ln_b[i],
                                    head_dim=A_HD2, scale=1.0 - lam_init)
            h_p = fin(o_p, g_p, h_p, tm=PROJ_ROWS)
            h_s = fin(o_s, g_s, h_s, tm=db)
            kv_shape = (batch, lp, A_HEADS, A_HD2)
            k_p_rows.append(k_p.reshape(kv_shape)[:, :length])
            v_p_rows.append(v_p.reshape(kv_shape)[:, :length])
            k_s_rows.append(k_s.reshape(db, 1, A_HEADS, A_HD2))
            v_s_rows.append(v_s.reshape(db, 1, A_HEADS, A_HD2))
        else:
            gi = i // 2
            extra = G_RANK_PAD - G_RANK
            w_in = jnp.pad(gla_w_in[gi], ((0, 0), (0, extra))).astype(BF16)
            w_up = jnp.pad(gla_w_gate_up[gi], ((0, extra), (0, 0))).astype(BF16)
            w_out = gla_w_out[gi].astype(BF16)
            b_gate = gla_b_gate[gi].reshape(1, G_QK).astype(F32)
            q_p, k_p, v_p, g_p, la_p = gla_proj(h_p, w_in, w_up, b_gate, PROJ_ROWS)
            q_s, k_s, v_s, g_s, la_s = gla_proj(h_s, w_in, w_up, b_gate, db)
            o_p, s_p = gla_prompt(q_p, k_p, v_p, la_p, batch, lp, length)
            o_s, s_s = gla_step(q_s, k_s, v_s, la_s, state_gla, gi)
            fin = functools.partial(finish, w_bf=w_out, norm_w=gla_norm_w[gi], ln_g=ln_g[i], ln_b=ln_b[i],
                                    head_dim=G_DV, scale=1.0)
            h_p = fin(o_p, g_p, h_p, tm=PROJ_ROWS)
            h_s = fin(o_s, g_s, h_s, tm=db)
            s_p_list.append(s_p)
            s_s_list.append(s_s)

    y_prompt = h_p.reshape(batch, lp, D_MODEL)[:, N_META:length]
    return (y_prompt, h_s.reshape(db, 1, D_MODEL),
            jnp.stack(k_p_rows, axis=1), jnp.stack(v_p_rows, axis=1), jnp.stack(s_p_list, axis=1),
            jnp.stack(k_s_rows, axis=1), jnp.stack(v_s_rows, axis=1), jnp.stack(s_s_list, axis=1))
```

```python
import functools
import math

import numpy as np
import jax
import jax.numpy as jnp
from jax import lax
from jax.experimental import pallas as pl
from jax.experimental.pallas import tpu as pltpu

F32 = jnp.float32
BF16 = jnp.bfloat16

D_MODEL = 1024
DEPTH = 4
N_META = 16
PAGE_SIZE = 128
A_HEADS = 8
A_HEAD_DIM = 64
A_HD2 = 2 * A_HEAD_DIM
A_QK = A_HEADS * A_HD2
G_HEADS = 4
G_DK = 128
G_DV = 256
G_QK = G_HEADS * G_DK
G_V = G_HEADS * G_DV
G_RANK = 16
G_RANK_PAD = 128
G_NORMALIZER = 16.0
G_CHUNK = 64
DN_ALPHA = (2.0 * DEPTH) ** 0.25
LN_EPS = 1e-5
RMS_EPS = 1e-5

MASK_VALUE = -1e30
LOG2E = math.log2(math.e)
ATT_TILE = 256
ATT_Q = 2 * ATT_TILE
VT_ROWS = A_HD2 + 16
GLA_ROWS = 128
PROJ_ROWS = 256
DEC_PAGES = 8
VMEM_LIMIT = 56 * 1024 * 1024

NT_DIMS = (((1,), (1,)), ((), ()))
TN_DIMS = (((0,), (0,)), ((), ()))


def _cparams(sem):
    return pltpu.CompilerParams(dimension_semantics=sem, vmem_limit_bytes=VMEM_LIMIT)


def _attn_proj_kernel(x_ref, w_ref, q_ref, k_ref, v_ref, g_ref, *flash_refs, tiles_per_seq):
    x = x_ref[...].astype(BF16)

    def cols(c):
        return jnp.dot(x, w_ref[:, c * A_QK:(c + 1) * A_QK], preferred_element_type=F32)

    q_ref[...] = (cols(0) * (A_HEAD_DIM ** -0.5 * LOG2E)).astype(BF16)
    k = cols(1)
    k_ref[...] = k
    v = cols(2)
    v_ref[...] = v
    g_ref[...] = cols(3)
    if not flash_refs:
        return
    ka_ref, vt_ref = flash_refs
    t = ATT_TILE
    tile = pl.program_id(0) % tiles_per_seq
    pos_lo = lax.broadcasted_iota(jnp.int32, (t, A_HD2), 0).astype(F32)
    lane = lax.broadcasted_iota(jnp.int32, (t, A_HD2), 1)
    parity = (tile % 2).astype(F32)
    pos_cols = jnp.where(lane < 3, pos_lo, jnp.where(lane < 6, parity, 0.0)).astype(BF16)
    sub = lax.broadcasted_iota(jnp.int32, (VT_ROWS - A_HD2, t), 0)
    ones_rows = jnp.where(sub == 0, 1.0, 0.0).astype(BF16)
    for h in range(A_HEADS):
        hs = slice(h * A_HD2, (h + 1) * A_HD2)
        ka_ref[:, 2 * h * A_HD2:(2 * h + 1) * A_HD2] = k[:, hs].astype(BF16)
        ka_ref[:, (2 * h + 1) * A_HD2:(2 * h + 2) * A_HD2] = pos_cols
        vt_ref[h, 0:A_HD2, :] = v[:, hs].T.astype(BF16)
        vt_ref[h, A_HD2:VT_ROWS, :] = ones_rows


def attn_proj(x, w_bf, tm, batch=None):
    m = x.shape[0]
    row = lambda i: (i, 0)
    blk = pl.BlockSpec((tm, A_QK), row)
    sds = lambda dt: jax.ShapeDtypeStruct((m, A_QK), dt)
    out_specs = [blk] * 4
    out_shape = [sds(BF16), sds(F32), sds(F32), sds(F32)]
    tiles_per_seq = 1
    if batch is not None:
        assert tm == ATT_TILE
        tiles_per_seq = m // batch // tm
        out_specs += [pl.BlockSpec((tm, 2 * A_QK), row),
                      pl.BlockSpec((None, A_HEADS, None, VT_ROWS, tm),
                                   lambda i: (i // tiles_per_seq, 0, i % tiles_per_seq, 0, 0))]
        out_shape += [jax.ShapeDtypeStruct((m, 2 * A_QK), BF16),
                      jax.ShapeDtypeStruct((batch, A_HEADS, tiles_per_seq, VT_ROWS, tm), BF16)]
    return pl.pallas_call(
        functools.partial(_attn_proj_kernel, tiles_per_seq=tiles_per_seq),
        grid=(m // tm,),
        in_specs=[pl.BlockSpec((tm, D_MODEL), row),
                  pl.BlockSpec((D_MODEL, 4 * A_QK), lambda i: (0, 0))],
        out_specs=out_specs,
        out_shape=out_shape,
        compiler_params=_cparams(("parallel",)),
        name="attn_proj",
    )(x, w_bf)


def _log_sigmoid(x):
    return jnp.minimum(x, 0.0) - jnp.log1p(jnp.exp(-jnp.abs(x)))


def _gla_proj_kernel(x_ref, w_ref, wr_ref, wup_ref, bg_ref, q_ref, k_ref, v_ref, g_ref, la_ref):
    x = x_ref[...].astype(BF16)

    def cols(a, b):
        return jnp.dot(x, w_ref[:, a:b], preferred_element_type=F32)

    q_ref[...] = cols(0, G_QK) * (G_DK ** -0.5)
    k_ref[...] = cols(G_QK, 2 * G_QK)
    v_ref[...] = cols(2 * G_QK, 2 * G_QK + G_V)
    g_ref[...] = cols(2 * G_QK + G_V, 2 * G_QK + 2 * G_V)
    r = jnp.dot(x, wr_ref[...], preferred_element_type=F32)
    gate = jnp.dot(r.astype(BF16), wup_ref[...], preferred_element_type=F32) + bg_ref[...]
    la_ref[...] = _log_sigmoid(gate) * (LOG2E / G_NORMALIZER)


def gla_proj(x, w_bf, wr_bf, wup_bf, b_gate, tm):
    m = x.shape[0]
    row = lambda i: (i, 0)
    full = lambda i: (0, 0)
    return pl.pallas_call(
        _gla_proj_kernel,
        grid=(m // tm,),
        in_specs=[pl.BlockSpec((tm, D_MODEL), row),
                  pl.BlockSpec((D_MODEL, 2 * G_QK + 2 * G_V), full),
                  pl.BlockSpec((D_MODEL, G_RANK_PAD), full),
                  pl.BlockSpec((G_RANK_PAD, G_QK), full),
                  pl.BlockSpec((1, G_QK), full)],
        out_specs=[pl.BlockSpec((tm, G_QK), row), pl.BlockSpec((tm, G_QK), row),
                   pl.BlockSpec((tm, G_V), row), pl.BlockSpec((tm, G_V), row),
                   pl.BlockSpec((tm, G_QK), row)],
        out_shape=[jax.ShapeDtypeStruct((m, G_QK), F32), jax.ShapeDtypeStruct((m, G_QK), F32),
                   jax.ShapeDtypeStruct((m, G_V), F32), jax.ShapeDtypeStruct((m, G_V), F32),
                   jax.ShapeDtypeStruct((m, G_QK), F32)],
        compiler_params=_cparams(("parallel",)),
        name="gla_proj",
    )(x, w_bf, wr_bf, wup_bf, b_gate)


def _finish_kernel(o_ref, g_ref, h_ref, w_ref, nw_ref, lg_ref, lb_ref, out_ref, *, head_dim, scale):
    o = o_ref[...]
    nw = nw_ref[...]
    parts = []
    for c in range(o.shape[1] // head_dim):
        oh = o[:, c * head_dim:(c + 1) * head_dim]
        ms = jnp.mean(oh * oh, axis=-1, keepdims=True)
        parts.append(oh * lax.rsqrt(ms + RMS_EPS) * nw)
    on = jnp.concatenate(parts, axis=1)
    if scale != 1.0:
        on = on * scale
    g = g_ref[...]
    y = on * (g * jax.nn.sigmoid(g))
    y = jnp.dot(y.astype(BF16), w_ref[...], preferred_element_type=F32)
    x = DN_ALPHA * h_ref[...] + y
    mu = jnp.mean(x, axis=-1, keepdims=True)
    xc = x - mu
    var = jnp.mean(xc * xc, axis=-1, keepdims=True)
    out_ref[...] = xc * lax.rsqrt(var + LN_EPS) * lg_ref[...] + lb_ref[...]


def finish(o, g, h, w_bf, norm_w, ln_g, ln_b, head_dim, scale, tm):
    m, n = o.shape
    row = lambda i: (i, 0)
    full = lambda i: (0, 0)
    return pl.pallas_call(
        functools.partial(_finish_kernel, head_dim=head_dim, scale=scale),
        grid=(m // tm,),
        in_specs=[pl.BlockSpec((tm, n), row), pl.BlockSpec((tm, n), row),
                  pl.BlockSpec((tm, D_MODEL), row),
                  pl.BlockSpec((n, D_MODEL), full),
                  pl.BlockSpec((1, head_dim), full),
                  pl.BlockSpec((1, D_MODEL), full), pl.BlockSpec((1, D_MODEL), full)],
        out_specs=pl.BlockSpec((tm, D_MODEL), row),
        out_shape=jax.ShapeDtypeStruct((m, D_MODEL), F32),
        compiler_params=_cparams(("parallel",)),
        name="finish",
    )(o, g, h, w_bf, norm_w.reshape(1, head_dim), ln_g.reshape(1, D_MODEL), ln_b.reshape(1, D_MODEL))


def _lambda_value(lam_ref, lam_init):
    lv = lam_ref[...]
    e1 = jnp.exp(jnp.sum(lv[0:1] * lv[1:2], axis=-1, keepdims=True))
    e2 = jnp.exp(jnp.sum(lv[2:3] * lv[3:4], axis=-1, keepdims=True))
    return e1 - e2 + lam_init


def _split_maps(q):
    lane = lax.broadcasted_iota(jnp.int32, q.shape, 1)
    zero = jnp.zeros_like(q)
    return jnp.concatenate([jnp.where(lane < A_HEAD_DIM, q, zero),
                            jnp.where(lane >= A_HEAD_DIM, q, zero)], axis=0)


def _bf16_terms(x):
    t1 = x.astype(BF16).astype(F32)
    r1 = x - t1
    t2 = r1.astype(BF16).astype(F32)
    t3 = (r1 - t2).astype(BF16).astype(F32)
    return t1, t2, t3


def _flash_kernel(c_ref, lam_ref, q_ref, ka_ref, vt_ref, o_ref, m_sc, acc_sc, sa_sc, ma_sc, sb_sc, mb_sc,
                  *, lam_init):
    t = ATT_TILE
    w = ATT_Q
    h = pl.program_id(1)
    qi = pl.program_id(2)
    c = c_ref[h]
    lane = lax.broadcasted_iota(jnp.int32, (1, A_HD2), 1)
    coef = jnp.where(lane < 3, c, jnp.where(lane < 6, c * t, 0.0))
    t1, t2, t3 = _bf16_terms(coef)
    part0 = (lane == 0) | (lane == 3)
    part1 = (lane == 1) | (lane == 4)
    coef = jnp.where(part0, t1, jnp.where(part1, t2, t3)).astype(BF16)
    qa = jnp.concatenate([_split_maps(q_ref[...]), jnp.broadcast_to(coef, (2 * w, A_HD2))], axis=1)

    def scores(u):
        start = pl.multiple_of(u * w, w)
        return lax.dot_general(ka_ref[pl.ds(start, w), :], qa, NT_DIMS,
                               preferred_element_type=F32)

    def prefetch(u, s_ref, smax_ref):
        s = scores(u)
        s_ref[...] = s
        smax_ref[...] = jnp.max(s, axis=0, keepdims=True)

    def accumulate(s, s_max, u):
        offset = c * (w * (u - qi)).astype(F32)
        m_old = m_sc[...]
        m_rel = jnp.maximum(m_old - offset, s_max)
        m_new = m_rel + offset
        alpha = jnp.exp2(m_old - m_new)
        p = jnp.exp2(s - m_rel).astype(BF16)
        pv = jnp.dot(vt_ref[2 * u], p[0:t], preferred_element_type=F32)
        pv += jnp.dot(vt_ref[2 * u + 1], p[t:w], preferred_element_type=F32)
        acc_sc[...] = alpha * acc_sc[...] + pv
        m_sc[...] = m_new

    def accumulate_diagonal(s):
        key = lax.broadcasted_iota(jnp.int32, (w, 2 * w), 0)
        col = lax.broadcasted_iota(jnp.int32, (w, 2 * w), 1)
        s = jnp.where(key <= jnp.where(col >= w, col - w, col), s, MASK_VALUE)
        accumulate(s, jnp.max(s, axis=0, keepdims=True), qi)

    m_sc[...] = jnp.full_like(m_sc, MASK_VALUE)
    acc_sc[...] = jnp.zeros_like(acc_sc)
    prefetch(0, sa_sc, ma_sc)

    def body(jj, carry):
        u = 2 * jj
        prefetch(u + 1, sb_sc, mb_sc)
        accumulate(sa_sc[...], ma_sc[...], u)
        prefetch(u + 2, sa_sc, ma_sc)
        accumulate(sb_sc[...], mb_sc[...], u + 1)
        return carry

    lax.fori_loop(0, qi // 2, body, 0)

    @pl.when(qi % 2 == 1)
    def _():
        s_diag = scores(qi)
        accumulate(sa_sc[...], ma_sc[...], qi - 1)
        accumulate_diagonal(s_diag)

    @pl.when(qi % 2 == 0)
    def _():
        accumulate_diagonal(sa_sc[...])

    lam = _lambda_value(lam_ref, lam_init)
    acc = acc_sc[...]
    o = acc[0:A_HD2] / acc[A_HD2:A_HD2 + 1]
    o_ref[...] = (o[:, :w] - lam * o[:, w:]).T


def flash_prompt(q, ka, vt, lam_rows, lam_init, batch, lp):
    w = ATT_Q
    nq = lp // w
    coefs = np.array([2.0 ** (-8.0 * (i + 1) / A_HEADS) * LOG2E for i in range(A_HEADS)], np.float32)
    return pl.pallas_call(
        functools.partial(_flash_kernel, lam_init=lam_init),
        grid=(batch, A_HEADS, nq),
        in_specs=[pl.BlockSpec(memory_space=pltpu.SMEM),
                  pl.BlockSpec((4, A_HEAD_DIM), lambda b, h, i: (0, 0)),
                  pl.BlockSpec((w, A_HD2), lambda b, h, i: (b * nq + i, h)),
                  pl.BlockSpec((lp, 2 * A_HD2), lambda b, h, i: (b, h)),
                  pl.BlockSpec((None, None, lp // ATT_TILE, VT_ROWS, ATT_TILE),
                               lambda b, h, i: (b, h, 0, 0, 0))],
        out_specs=pl.BlockSpec((w, A_HD2), lambda b, h, i: (b * nq + i, h)),
        out_shape=jax.ShapeDtypeStruct((batch * lp, A_QK), F32),
        scratch_shapes=[pltpu.VMEM((1, 2 * w), F32), pltpu.VMEM((VT_ROWS, 2 * w), F32),
                        pltpu.VMEM((w, 2 * w), F32), pltpu.VMEM((1, 2 * w), F32),
                        pltpu.VMEM((w, 2 * w), F32), pltpu.VMEM((1, 2 * w), F32)],
        compiler_params=_cparams(("parallel", "parallel", "arbitrary")),
        name="flash_prompt",
    )(jnp.asarray(coefs), lam_rows, q, ka, vt)


def _decode_kernel(pt_ref, lam_ref, q_ref, kn_ref, vn_ref, bias_ref, slope_ref, *rest, lam_init, past_len):
    del pt_ref
    k_refs = rest[:DEC_PAGES]
    v_refs = rest[DEC_PAGES:2 * DEC_PAGES]
    o_ref, m_sc, l_sc, acc_sc = rest[2 * DEC_PAGES:]
    j = pl.program_id(1)
    qq = _split_maps(q_ref[...]).astype(BF16)
    slope_col = slope_ref[...]

    @pl.when(j == 0)
    def _():
        kn = kn_ref[...]
        kn2 = jnp.concatenate([kn, kn], axis=0)
        m_sc[...] = jnp.sum(qq.astype(F32) * kn2, axis=-1, keepdims=True)
        l_sc[...] = jnp.ones_like(l_sc)
        vn = vn_ref[...]
        acc_sc[...] = jnp.concatenate([vn, vn], axis=0)

    bias = bias_ref[...]
    width = PAGE_SIZE * A_HEADS
    s_pages = []
    for i in range(DEC_PAGES):
        page = j * DEC_PAGES + i
        k = k_refs[i][...].reshape(width, A_HD2).astype(BF16)
        offset = slope_col * (page * PAGE_SIZE - past_len).astype(F32)
        s_pages.append(lax.dot_general(qq, k, NT_DIMS, preferred_element_type=F32) + (bias + offset))
    m_old = m_sc[...]
    m_new = m_old
    for s in s_pages:
        m_new = jnp.maximum(m_new, jnp.max(s, axis=-1, keepdims=True))
    alpha = jnp.exp2(m_old - m_new)
    l_new = alpha * l_sc[...]
    acc = alpha * acc_sc[...]
    for i, s in enumerate(s_pages):
        p = jnp.exp2(s - m_new)
        l_new += jnp.sum(p, axis=-1, keepdims=True)
        v = v_refs[i][...].reshape(width, A_HD2).astype(BF16)
        acc += jnp.dot(p.astype(BF16), v, preferred_element_type=F32)
    l_sc[...] = l_new
    acc_sc[...] = acc
    m_sc[...] = m_new

    @pl.when(j == pl.num_programs(1) - 1)
    def _():
        lam = _lambda_value(lam_ref, lam_init)
        o = acc_sc[...] / l_sc[...]
        o_ref[...] = o[:A_HEADS] - lam * o[A_HEADS:]


def decode_attn(q_s, k_s, v_s, cache_k, cache_v, page_table, layer, lam_rows, lam_init):
    db = q_s.shape[0]
    n_pages = page_table.shape[1]
    past_len = n_pages * PAGE_SIZE
    slopes = np.array([2.0 ** (-8.0 * (i + 1) / A_HEADS) * LOG2E for i in range(A_HEADS)], np.float32)
    rows_h = np.tile(np.arange(A_HEADS), 2)
    cols_t = np.repeat(np.arange(PAGE_SIZE), A_HEADS)
    cols_h = np.tile(np.arange(A_HEADS), PAGE_SIZE)
    bias = np.where(rows_h[:, None] == cols_h[None, :],
                    slopes[rows_h][:, None] * cols_t[None, :].astype(np.float32),
                    np.float32(MASK_VALUE)).astype(np.float32)
    slope_col = slopes[rows_h][:, None]

    hd = lambda a: a.reshape(db, A_HEADS, A_HD2)
    per_b = pl.BlockSpec((None, A_HEADS, A_HD2), lambda b, j, pt: (b, 0, 0))
    const2 = lambda b, j, pt: (0, 0)

    def page_spec(i):
        return pl.BlockSpec((None, None, PAGE_SIZE, A_HEADS, A_HD2),
                            lambda b, j, pt: (pt[b, j * DEC_PAGES + i], layer, 0, 0, 0))

    grid_spec = pltpu.PrefetchScalarGridSpec(
        num_scalar_prefetch=1,
        grid=(db, n_pages // DEC_PAGES),
        in_specs=[pl.BlockSpec((4, A_HEAD_DIM), const2), per_b, per_b, per_b,
                  pl.BlockSpec((2 * A_HEADS, PAGE_SIZE * A_HEADS), const2),
                  pl.BlockSpec((2 * A_HEADS, 1), const2)]
                 + [page_spec(i) for i in range(DEC_PAGES)] * 2,
        out_specs=per_b,
        scratch_shapes=[pltpu.VMEM((2 * A_HEADS, 1), F32), pltpu.VMEM((2 * A_HEADS, 1), F32),
                        pltpu.VMEM((2 * A_HEADS, A_HD2), F32)],
    )
    out = pl.pallas_call(
        functools.partial(_decode_kernel, lam_init=lam_init, past_len=past_len),
        grid_spec=grid_spec,
        out_shape=jax.ShapeDtypeStruct((db, A_HEADS, A_HD2), F32),
        compiler_params=_cparams(("parallel", "arbitrary")),
        name="decode_attn",
    )(page_table, lam_rows, hd(q_s.astype(F32)), hd(k_s), hd(v_s), jnp.asarray(bias), jnp.asarray(slope_col),
      *([cache_k] * DEC_PAGES), *([cache_v] * DEC_PAGES))
    return out.reshape(db, A_QK)


G_LEVELS = tuple(G_CHUNK >> (i + 1) for i in range(G_CHUNK.bit_length() - 1))


def _gla_decay_matrix():
    c = G_CHUNK
    r = np.arange(c)[:, None]
    j = np.arange(c)[None, :]
    blocks = [j <= r, j > r]
    for blk in G_LEVELS:
        ref = (r // (2 * blk)) * (2 * blk) + blk
        blocks.append(np.where(r >= ref, (j > ref) & (j <= r), (j > r) & (j <= ref)))
    return np.concatenate(blocks, axis=0).astype(np.float32)


def _gla_level_masks():
    c = G_CHUNK
    n = G_HEADS * c
    row = lax.broadcasted_iota(jnp.int32, (c, 1), 0)
    t_i = lax.broadcasted_iota(jnp.int32, (n, n), 0)
    s_i = lax.broadcasted_iota(jnp.int32, (n, n), 1)
    later, owns = [], []
    for blk in G_LEVELS:
        shift = blk.bit_length()
        later.append((row & blk) != 0)
        owns.append(((t_i >> shift) == (s_i >> shift)) & ((t_i & blk) != 0) & ((s_i & blk) == 0))
    return later, owns, t_i == s_i


def _heads_to_rows(x, width):
    return jnp.concatenate([x[:, h * width:(h + 1) * width] for h in range(G_HEADS)], axis=0)


def _gla_chunk(q, k, v, la, states, dmat, masks):
    c = q.shape[0]
    later, owns, diagonal = masks
    l1 = la.astype(BF16)
    l2 = (la - l1.astype(F32)).astype(BF16)
    e = jnp.dot(dmat, l1, preferred_element_type=F32) + jnp.dot(dmat, l2, preferred_element_type=F32)
    b = e[0:c]
    vb = v.astype(BF16)

    qe = (q * jnp.exp2(b)).astype(BF16)
    qk = _heads_to_rows(q * k, G_DK)
    a = jnp.where(diagonal, jnp.sum(qk, axis=-1, keepdims=True), 0.0)
    for i in range(len(G_LEVELS)):
        x = (jnp.where(later[i], q, k) * jnp.exp2(e[(2 + i) * c:(3 + i) * c])).astype(BF16)
        xs = _heads_to_rows(x, G_DK)
        a = jnp.where(owns[i], lax.dot_general(xs, xs, NT_DIMS, preferred_element_type=F32), a)
    o_intra = jnp.dot(a.astype(BF16), _heads_to_rows(vb, G_DV), preferred_element_type=F32)

    k_dec = (k * jnp.exp2(e[c:2 * c])).astype(BF16)
    decay = jnp.exp2(b[c - 1:c])
    outs, new_states = [], []
    for h in range(G_HEADS):
        ks = slice(h * G_DK, (h + 1) * G_DK)
        st = states[h]
        o_inter = lax.dot_general(qe[:, ks], st.astype(BF16), NT_DIMS, preferred_element_type=F32)
        outs.append(o_inter + o_intra[h * c:(h + 1) * c])
        new_states.append(st * decay[:, ks] + lax.dot_general(
            vb[:, h * G_DV:(h + 1) * G_DV], k_dec[:, ks], TN_DIMS, preferred_element_type=F32))
    return outs, new_states


def _gla_kernel(dmat_ref, q_ref, k_ref, v_ref, la_ref, o_ref, s_ref, st_sc, *, valid_len):
    i = pl.program_id(1)

    @pl.when(i == 0)
    def _():
        st_sc[...] = jnp.zeros_like(st_sc)

    dmat = dmat_ref[...]
    masks = _gla_level_masks()
    states = [st_sc[h] for h in range(G_HEADS)]
    for c in range(GLA_ROWS // G_CHUNK):
        sl = slice(c * G_CHUNK, (c + 1) * G_CHUNK)
        pos = i * GLA_ROWS + c * G_CHUNK + lax.broadcasted_iota(jnp.int32, (G_CHUNK, 1), 0)
        valid = pos < valid_len
        la = jnp.where(valid, la_ref[sl, :], 0.0)
        k = jnp.where(valid, k_ref[sl, :], 0.0)
        outs, states = _gla_chunk(q_ref[sl, :], k, v_ref[sl, :], la, states, dmat, masks)
        for h in range(G_HEADS):
            o_ref[sl, h * G_DV:(h + 1) * G_DV] = outs[h]
    for h in range(G_HEADS):
        st_sc[h] = states[h]

    @pl.when(i == pl.num_programs(1) - 1)
    def _():
        for h in range(G_HEADS):
            s_ref[h] = states[h].T


def gla_prompt(q, k, v, la, batch, lp, valid_len):
    nblk = lp // GLA_ROWS
    qk_spec = pl.BlockSpec((GLA_ROWS, G_QK), lambda b, i: (b * nblk + i, 0))
    v_spec = pl.BlockSpec((GLA_ROWS, G_V), lambda b, i: (b * nblk + i, 0))
    dmat = jnp.asarray(_gla_decay_matrix(), BF16)
    return pl.pallas_call(
        functools.partial(_gla_kernel, valid_len=valid_len),
        grid=(batch, nblk),
        in_specs=[pl.BlockSpec(dmat.shape, lambda b, i: (0, 0)), qk_spec, qk_spec, v_spec, qk_spec],
        out_specs=[v_spec, pl.BlockSpec((None, G_HEADS, G_DK, G_DV), lambda b, i: (b, 0, 0, 0))],
        out_shape=[jax.ShapeDtypeStruct((batch * lp, G_V), F32),
                   jax.ShapeDtypeStruct((batch, G_HEADS, G_DK, G_DV), F32)],
        scratch_shapes=[pltpu.VMEM((G_HEADS, G_DV, G_DK), F32)],
        compiler_params=_cparams(("parallel", "arbitrary")),
        name="gla_prompt",
    )(dmat, q, k, v, la)


def _stack_rows(rows, n_rows=16):
    n = rows[0].shape[1]
    idx = lax.broadcasted_iota(jnp.int32, (n_rows, n), 0)
    out = jnp.zeros((n_rows, n), F32)
    for r, x in enumerate(rows):
        out = jnp.where(idx == r, x, out)
    return out.astype(BF16)


def _gla_step_kernel(q_ref, k_ref, v_ref, la_ref, s_ref, o_ref, so_ref):
    q = q_ref[...]
    k = k_ref[...]
    v = v_ref[...]
    a = jnp.exp2(la_ref[...])
    ones = jnp.ones((16, G_DV), BF16)
    o_parts = []
    for h in range(G_HEADS):
        ks = slice(h * G_DK, (h + 1) * G_DK)
        vs = slice(h * G_DV, (h + 1) * G_DV)
        a_h = a[:, ks]
        a1 = a_h.astype(BF16)
        r1 = a_h - a1.astype(F32)
        a2 = r1.astype(BF16)
        a3 = (r1 - a2.astype(F32)).astype(BF16)
        a_rows = _stack_rows([a1.astype(F32), a2.astype(F32), a3.astype(F32)])
        a_col = lax.dot_general(a_rows, ones, TN_DIMS, preferred_element_type=F32)
        kv = lax.dot_general(_stack_rows([k[:, ks]]), _stack_rows([v[:, vs]]), TN_DIMS,
                             preferred_element_type=F32)
        s1 = a_col * s_ref[h] + kv
        so_ref[h] = s1
        o = jnp.dot(_stack_rows([q[:, ks]]), s1.astype(BF16), preferred_element_type=F32)
        o_parts.append(o[0:1])
    o_ref[...] = jnp.concatenate(o_parts, axis=1)


def gla_step(q_s, k_s, v_s, la_s, state_gla, layer):
    db = q_s.shape[0]
    r3 = lambda a: a.reshape(db, 1, a.shape[1])
    vec = lambda n: pl.BlockSpec((None, 1, n), lambda b: (b, 0, 0))
    o, s_new = pl.pallas_call(
        _gla_step_kernel,
        grid=(db,),
        in_specs=[vec(G_QK), vec(G_QK), vec(G_V), vec(G_QK),
                  pl.BlockSpec((None, None, G_HEADS, G_DK, G_DV), lambda b: (b, layer, 0, 0, 0))],
        out_specs=[vec(G_V), pl.BlockSpec((None, G_HEADS, G_DK, G_DV), lambda b: (b, 0, 0, 0))],
        out_shape=[jax.ShapeDtypeStruct((db, 1, G_V), F32),
                   jax.ShapeDtypeStruct((db, G_HEADS, G_DK, G_DV), F32)],
        compiler_params=_cparams(("parallel",)),
        name="gla_step",
    )(r3(q_s), r3(k_s), r3(v_s), r3(la_s), state_gla)
    return o.reshape(db, G_V), s_new


def kernel(x_prompt, x_sample, cache_k, cache_v, state_gla, page_table, meta_tokens, attn_w_in, attn_lq1, attn_lk1, attn_lq2, attn_lk2, attn_subln_w, attn_w_out, gla_w_in, gla_w_gate_up, gla_b_gate, gla_norm_w, gla_w_out, ln_g, ln_b):
    batch, seq, _ = x_prompt.shape
    db = x_sample.shape[0]
    length = N_META + seq
    lp = -(-length // ATT_Q) * ATT_Q
    meta = jnp.broadcast_to(meta_tokens[None].astype(x_prompt.dtype), (batch, N_META, D_MODEL))
    pad = jnp.zeros((batch, lp - length, D_MODEL), x_prompt.dtype)
    h_p = jnp.concatenate([meta, x_prompt, pad], axis=1).reshape(batch * lp, D_MODEL)
    h_s = x_sample.reshape(db, D_MODEL)

    k_p_rows, v_p_rows, k_s_rows, v_s_rows, s_p_list, s_s_list = [], [], [], [], [], []
    for i in range(DEPTH):
        if i % 2 == 0:
            a = i // 2
            lam_init = 0.8 - 0.6 * math.exp(-0.3 * i)
            w_in = attn_w_in[a].astype(BF16)
            w_out = attn_w_out[a].astype(BF16)
            lam_rows = jnp.stack([attn_lq1[a], attn_lk1[a], attn_lq2[a], attn_lk2[a]]).astype(F32)
            q_p, k_p, v_p, g_p, ka_p, vt_p = attn_proj(h_p, w_in, ATT_TILE, batch)
            q_s, k_s, v_s, g_s = attn_proj(h_s, w_in, db)
            o_p = flash_prompt(q_p, ka_p, vt_p, lam_rows, lam_init, batch, lp)
            o_s = decode_attn(q_s, k_s, v_s, cache_k, cache_v, page_table, a, lam_rows, lam_init)
            fin = functools.partial(finish, w_bf=w_out, norm_w=attn_subln_w[a], ln_g=ln_g[i], ln_b=ln_b[i],
                                    head_dim=A_HD2, scale=1.0 - lam_init)
            h_p = fin(o_p, g_p, h_p, tm=PROJ_ROWS)
            h_s = fin(o_s, g_s, h_s, tm=db)
            kv_shape = (batch, lp, A_HEADS, A_HD2)
            k_p_rows.append(k_p.reshape(kv_shape)[:, :length])
            v_p_rows.append(v_p.reshape(kv_shape)[:, :length])
            k_s_rows.append(k_s.reshape(db, 1, A_HEADS, A_HD2))
            v_s_rows.append(v_s.reshape(db, 1, A_HEADS, A_HD2))
        else:
            gi = i // 2
            extra = G_RANK_PAD - G_RANK
            n_main = 2 * G_QK + 2 * G_V
            w_in = gla_w_in[gi][:, :n_main].astype(BF16)
            w_rank = jnp.pad(gla_w_in[gi][:, n_main:], ((0, 0), (0, extra))).astype(BF16)
            w_up = jnp.pad(gla_w_gate_up[gi], ((0, extra), (0, 0))).astype(BF16)
            w_out = gla_w_out[gi].astype(BF16)
            b_gate = gla_b_gate[gi].reshape(1, G_QK).astype(F32)
            q_p, k_p, v_p, g_p, la_p = gla_proj(h_p, w_in, w_rank, w_up, b_gate, PROJ_ROWS)
            q_s, k_s, v_s, g_s, la_s = gla_proj(h_s, w_in, w_rank, w_up, b_gate, db)
            o_p, s_p = gla_prompt(q_p, k_p, v_p, la_p, batch, lp, length)
            o_s, s_s = gla_step(q_s, k_s, v_s, la_s, state_gla, gi)
            fin = functools.partial(finish, w_bf=w_out, norm_w=gla_norm_w[gi], ln_g=ln_g[i], ln_b=ln_b[i],
                                    head_dim=G_DV, scale=1.0)
            h_p = fin(o_p, g_p, h_p, tm=PROJ_ROWS)
            h_s = fin(o_s, g_s, h_s, tm=db)
            s_p_list.append(s_p)
            s_s_list.append(s_s)

    y_prompt = h_p.reshape(batch, lp, D_MODEL)[:, N_META:length]
    return (y_prompt, h_s.reshape(db, 1, D_MODEL),
            jnp.stack(k_p_rows, axis=1), jnp.stack(v_p_rows, axis=1), jnp.stack(s_p_list, axis=1),
            jnp.stack(k_s_rows, axis=1), jnp.stack(v_s_rows, axis=1), jnp.stack(s_s_list, axis=1))
```

```python
import functools
import math

import numpy as np
import jax
import jax.numpy as jnp
from jax import lax
from jax.experimental import pallas as pl
from jax.experimental.pallas import tpu as pltpu

F32 = jnp.float32
BF16 = jnp.bfloat16

D_MODEL = 1024
DEPTH = 4
N_META = 16
PAGE_SIZE = 128
A_HEADS = 8
A_HEAD_DIM = 64
A_HD2 = 2 * A_HEAD_DIM
A_QK = A_HEADS * A_HD2
G_HEADS = 4
G_DK = 128
G_DV = 256
G_QK = G_HEADS * G_DK
G_V = G_HEADS * G_DV
G_RANK = 16
G_RANK_PAD = 128
G_NORMALIZER = 16.0
G_CHUNK = 64
DN_ALPHA = (2.0 * DEPTH) ** 0.25
LN_EPS = 1e-5
RMS_EPS = 1e-5

MASK_VALUE = -1e30
LOG2E = math.log2(math.e)
ATT_TILE = 256
ATT_Q = 2 * ATT_TILE
VT_ROWS = A_HD2 + 16
GLA_ROWS = 128
PROJ_ROWS = 256
DEC_PAGES = 8
VMEM_LIMIT = 56 * 1024 * 1024

NT_DIMS = (((1,), (1,)), ((), ()))
TN_DIMS = (((0,), (0,)), ((), ()))


def _cparams(sem):
    return pltpu.CompilerParams(dimension_semantics=sem, vmem_limit_bytes=VMEM_LIMIT)


def _attn_proj_kernel(x_ref, w_ref, q_ref, k_ref, v_ref, g_ref, *flash_refs, tiles_per_seq, last_kv_tile):
    x = x_ref[...].astype(BF16)

    def cols(c):
        return jnp.dot(x, w_ref[:, c * A_QK:(c + 1) * A_QK], preferred_element_type=F32)

    q_ref[...] = (cols(0) * (A_HEAD_DIM ** -0.5 * LOG2E)).astype(BF16)
    k = cols(1)
    v = cols(2)
    g_ref[...] = cols(3)
    if not flash_refs:
        k_ref[...] = k
        v_ref[...] = v
        return
    ka_ref, vt_ref = flash_refs
    t = ATT_TILE
    tile = pl.program_id(0) % tiles_per_seq

    @pl.when(tile <= last_kv_tile)
    def _():
        k_ref[...] = k
        v_ref[...] = v

    pos_lo = lax.broadcasted_iota(jnp.int32, (t, A_HD2), 0).astype(F32)
    lane = lax.broadcasted_iota(jnp.int32, (t, A_HD2), 1)
    parity = (tile % 2).astype(F32)
    pos_cols = jnp.where(lane < 3, pos_lo, jnp.where(lane < 6, parity, 0.0)).astype(BF16)
    sub = lax.broadcasted_iota(jnp.int32, (VT_ROWS - A_HD2, t), 0)
    ones_rows = jnp.where(sub == 0, 1.0, 0.0).astype(BF16)
    for h in range(A_HEADS):
        hs = slice(h * A_HD2, (h + 1) * A_HD2)
        ka_ref[:, 2 * h * A_HD2:(2 * h + 1) * A_HD2] = k[:, hs].astype(BF16)
        ka_ref[:, (2 * h + 1) * A_HD2:(2 * h + 2) * A_HD2] = pos_cols
        vt_ref[h, 0:A_HD2, :] = v[:, hs].T.astype(BF16)
        vt_ref[h, A_HD2:VT_ROWS, :] = ones_rows


def attn_proj(x, w_bf, tm, batch=None, valid_len=None):
    m = x.shape[0]
    row = lambda i: (i, 0)
    blk = pl.BlockSpec((tm, A_QK), row)
    sds = lambda dt: jax.ShapeDtypeStruct((m, A_QK), dt)
    if batch is None:
        tiles_per_seq, last_kv_tile = 1, 0
        out_specs = [blk] * 4
        out_shape = [sds(BF16), sds(F32), sds(F32), sds(F32)]
    else:
        assert tm == ATT_TILE
        tiles_per_seq = m // batch // tm
        last_kv_tile = (valid_len - 1) // tm
        kv_blk = pl.BlockSpec((None, tm, A_QK), lambda i: (i // tiles_per_seq,
                                                           jnp.minimum(i % tiles_per_seq, last_kv_tile), 0))
        kv_sds = jax.ShapeDtypeStruct((batch, valid_len, A_QK), F32)
        out_specs = [blk, kv_blk, kv_blk, blk,
                     pl.BlockSpec((tm, 2 * A_QK), row),
                     pl.BlockSpec((None, A_HEADS, None, VT_ROWS, tm),
                                  lambda i: (i // tiles_per_seq, 0, i % tiles_per_seq, 0, 0))]
        out_shape = [sds(BF16), kv_sds, kv_sds, sds(F32),
                     jax.ShapeDtypeStruct((m, 2 * A_QK), BF16),
                     jax.ShapeDtypeStruct((batch, A_HEADS, tiles_per_seq, VT_ROWS, tm), BF16)]
    return pl.pallas_call(
        functools.partial(_attn_proj_kernel, tiles_per_seq=tiles_per_seq, last_kv_tile=last_kv_tile),
        grid=(m // tm,),
        in_specs=[pl.BlockSpec((tm, D_MODEL), row),
                  pl.BlockSpec((D_MODEL, 4 * A_QK), lambda i: (0, 0))],
        out_specs=out_specs,
        out_shape=out_shape,
        compiler_params=_cparams(("arbitrary",)),
        name="attn_proj",
    )(x, w_bf)


def _log_sigmoid(x):
    return jnp.minimum(x, 0.0) - jnp.log1p(jnp.exp(-jnp.abs(x)))


def _gla_proj_kernel(x_ref, w_ref, wr_ref, wup_ref, bg_ref, q_ref, k_ref, v_ref, g_ref, la_ref):
    x = x_ref[...].astype(BF16)

    def cols(a, b):
        return jnp.dot(x, w_ref[:, a:b], preferred_element_type=F32)

    q_ref[...] = cols(0, G_QK) * (G_DK ** -0.5)
    k_ref[...] = cols(G_QK, 2 * G_QK)
    v_ref[...] = cols(2 * G_QK, 2 * G_QK + G_V)
    g_ref[...] = cols(2 * G_QK + G_V, 2 * G_QK + 2 * G_V)
    r = jnp.dot(x, wr_ref[...], preferred_element_type=F32)
    gate = jnp.dot(r.astype(BF16), wup_ref[...], preferred_element_type=F32) + bg_ref[...]
    la_ref[...] = _log_sigmoid(gate) * (LOG2E / G_NORMALIZER)


def gla_proj(x, w_bf, wr_bf, wup_bf, b_gate, tm):
    m = x.shape[0]
    row = lambda i: (i, 0)
    full = lambda i: (0, 0)
    return pl.pallas_call(
        _gla_proj_kernel,
        grid=(m // tm,),
        in_specs=[pl.BlockSpec((tm, D_MODEL), row),
                  pl.BlockSpec((D_MODEL, 2 * G_QK + 2 * G_V), full),
                  pl.BlockSpec((D_MODEL, G_RANK_PAD), full),
                  pl.BlockSpec((G_RANK_PAD, G_QK), full),
                  pl.BlockSpec((1, G_QK), full)],
        out_specs=[pl.BlockSpec((tm, G_QK), row), pl.BlockSpec((tm, G_QK), row),
                   pl.BlockSpec((tm, G_V), row), pl.BlockSpec((tm, G_V), row),
                   pl.BlockSpec((tm, G_QK), row)],
        out_shape=[jax.ShapeDtypeStruct((m, G_QK), F32), jax.ShapeDtypeStruct((m, G_QK), F32),
                   jax.ShapeDtypeStruct((m, G_V), F32), jax.ShapeDtypeStruct((m, G_V), F32),
                   jax.ShapeDtypeStruct((m, G_QK), F32)],
        compiler_params=_cparams(("parallel",)),
        name="gla_proj",
    )(x, w_bf, wr_bf, wup_bf, b_gate)


def _finish_kernel(o_ref, g_ref, h_ref, w_ref, nw_ref, lg_ref, lb_ref, out_ref, *, head_dim, scale):
    o = o_ref[...]
    nw = nw_ref[...]
    parts = []
    for c in range(o.shape[1] // head_dim):
        oh = o[:, c * head_dim:(c + 1) * head_dim]
        ms = jnp.mean(oh * oh, axis=-1, keepdims=True)
        parts.append(oh * lax.rsqrt(ms + RMS_EPS) * nw)
    on = jnp.concatenate(parts, axis=1)
    if scale != 1.0:
        on = on * scale
    g = g_ref[...]
    y = on * (g * jax.nn.sigmoid(g))
    y = jnp.dot(y.astype(BF16), w_ref[...], preferred_element_type=F32)
    x = DN_ALPHA * h_ref[...] + y
    mu = jnp.mean(x, axis=-1, keepdims=True)
    xc = x - mu
    var = jnp.mean(xc * xc, axis=-1, keepdims=True)
    out_ref[...] = xc * lax.rsqrt(var + LN_EPS) * lg_ref[...] + lb_ref[...]


def finish(o, g, h, w_bf, norm_w, ln_g, ln_b, head_dim, scale, tm):
    m, n = o.shape
    row = lambda i: (i, 0)
    full = lambda i: (0, 0)
    return pl.pallas_call(
        functools.partial(_finish_kernel, head_dim=head_dim, scale=scale),
        grid=(m // tm,),
        in_specs=[pl.BlockSpec((tm, n), row), pl.BlockSpec((tm, n), row),
                  pl.BlockSpec((tm, D_MODEL), row),
                  pl.BlockSpec((n, D_MODEL), full),
                  pl.BlockSpec((1, head_dim), full),
                  pl.BlockSpec((1, D_MODEL), full), pl.BlockSpec((1, D_MODEL), full)],
        out_specs=pl.BlockSpec((tm, D_MODEL), row),
        out_shape=jax.ShapeDtypeStruct((m, D_MODEL), F32),
        compiler_params=_cparams(("parallel",)),
        name="finish",
    )(o, g, h, w_bf, norm_w.reshape(1, head_dim), ln_g.reshape(1, D_MODEL), ln_b.reshape(1, D_MODEL))


def _lambda_value(lam_ref, lam_init):
    lv = lam_ref[...]
    e1 = jnp.exp(jnp.sum(lv[0:1] * lv[1:2], axis=-1, keepdims=True))
    e2 = jnp.exp(jnp.sum(lv[2:3] * lv[3:4], axis=-1, keepdims=True))
    return e1 - e2 + lam_init


def _split_maps(q):
    lane = lax.broadcasted_iota(jnp.int32, q.shape, 1)
    zero = jnp.zeros_like(q)
    return jnp.concatenate([jnp.where(lane < A_HEAD_DIM, q, zero),
                            jnp.where(lane >= A_HEAD_DIM, q, zero)], axis=0)


def _bf16_terms(x):
    t1 = x.astype(BF16).astype(F32)
    r1 = x - t1
    t2 = r1.astype(BF16).astype(F32)
    t3 = (r1 - t2).astype(BF16).astype(F32)
    return t1, t2, t3


def _flash_kernel(c_ref, lam_ref, q_ref, ka_ref, vt_ref, o_ref, m_sc, acc_sc, sa_sc, ma_sc, sb_sc, mb_sc,
                  *, lam_init, last_rows):
    t = ATT_TILE
    w = ATT_Q
    h = pl.program_id(1)
    qi = pl.program_id(2)
    c = c_ref[h]
    lane = lax.broadcasted_iota(jnp.int32, (1, A_HD2), 1)
    coef = jnp.where(lane < 3, c, jnp.where(lane < 6, c * t, 0.0))
    t1, t2, t3 = _bf16_terms(coef)
    part0 = (lane == 0) | (lane == 3)
    part1 = (lane == 1) | (lane == 4)
    coef = jnp.where(part0, t1, jnp.where(part1, t2, t3)).astype(BF16)

    def attend(nq_rows):
        cols = 2 * nq_rows
        qa = jnp.concatenate([_split_maps(q_ref[0:nq_rows, :]),
                              jnp.broadcast_to(coef, (cols, A_HD2))], axis=1)

        def scores(u):
            start = pl.multiple_of(u * w, w)
            return lax.dot_general(ka_ref[pl.ds(start, w), :], qa, NT_DIMS,
                                   preferred_element_type=F32)

        def prefetch(u, s_ref, smax_ref):
            s = scores(u)
            s_ref[:, 0:cols] = s
            smax_ref[:, 0:cols] = jnp.max(s, axis=0, keepdims=True)

        def accumulate(s, s_max, u):
            offset = c * (w * (u - qi)).astype(F32)
            m_old = m_sc[:, 0:cols]
            m_rel = jnp.maximum(m_old - offset, s_max)
            m_new = m_rel + offset
            alpha = jnp.exp2(m_old - m_new)
            p = jnp.exp2(s - m_rel).astype(BF16)
            pv = jnp.dot(vt_ref[2 * u], p[0:t], preferred_element_type=F32)
            pv += jnp.dot(vt_ref[2 * u + 1], p[t:w], preferred_element_type=F32)
            acc_sc[:, 0:cols] = alpha * acc_sc[:, 0:cols] + pv
            m_sc[:, 0:cols] = m_new

        def accumulate_diagonal(s):
            key = lax.broadcasted_iota(jnp.int32, (w, cols), 0)
            col = lax.broadcasted_iota(jnp.int32, (w, cols), 1)
            s = jnp.where(key <= jnp.where(col >= nq_rows, col - nq_rows, col), s, MASK_VALUE)
            accumulate(s, jnp.max(s, axis=0, keepdims=True), qi)

        m_sc[...] = jnp.full_like(m_sc, MASK_VALUE)
        acc_sc[...] = jnp.zeros_like(acc_sc)
        prefetch(0, sa_sc, ma_sc)

        def body(jj, carry):
            u = 2 * jj
            prefetch(u + 1, sb_sc, mb_sc)
            accumulate(sa_sc[:, 0:cols], ma_sc[:, 0:cols], u)
            prefetch(u + 2, sa_sc, ma_sc)
            accumulate(sb_sc[:, 0:cols], mb_sc[:, 0:cols], u + 1)
            return carry

        lax.fori_loop(0, qi // 2, body, 0)

        @pl.when(qi % 2 == 1)
        def _():
            s_diag = scores(qi)
            accumulate(sa_sc[:, 0:cols], ma_sc[:, 0:cols], qi - 1)
            accumulate_diagonal(s_diag)

        @pl.when(qi % 2 == 0)
        def _():
            accumulate_diagonal(sa_sc[:, 0:cols])

        lam = _lambda_value(lam_ref, lam_init)
        acc = acc_sc[:, 0:cols]
        o = acc[0:A_HD2] / acc[A_HD2:A_HD2 + 1]
        o_ref[0:nq_rows, :] = (o[:, :nq_rows] - lam * o[:, nq_rows:]).T
        if nq_rows < w:
            o_ref[nq_rows:w, :] = jnp.zeros((w - nq_rows, A_HD2), F32)

    if last_rows == w:
        attend(w)
    else:
        is_last = qi == pl.num_programs(2) - 1
        pl.when(jnp.logical_not(is_last))(lambda: attend(w))
        pl.when(is_last)(lambda: attend(last_rows))


def flash_prompt(q, ka, vt, lam_rows, lam_init, batch, lp, valid_len):
    w = ATT_Q
    nq = lp // w
    lane_tile = A_HD2
    last_rows = min(w, -(-(valid_len - (nq - 1) * w) // lane_tile) * lane_tile)
    coefs = np.array([2.0 ** (-8.0 * (i + 1) / A_HEADS) * LOG2E for i in range(A_HEADS)], np.float32)
    return pl.pallas_call(
        functools.partial(_flash_kernel, lam_init=lam_init, last_rows=last_rows),
        grid=(batch, A_HEADS, nq),
        in_specs=[pl.BlockSpec(memory_space=pltpu.SMEM),
                  pl.BlockSpec((4, A_HEAD_DIM), lambda b, h, i: (0, 0)),
                  pl.BlockSpec((w, A_HD2), lambda b, h, i: (b * nq + i, h)),
                  pl.BlockSpec((lp, 2 * A_HD2), lambda b, h, i: (b, h)),
                  pl.BlockSpec((None, None, lp // ATT_TILE, VT_ROWS, ATT_TILE),
                               lambda b, h, i: (b, h, 0, 0, 0))],
        out_specs=pl.BlockSpec((w, A_HD2), lambda b, h, i: (b * nq + i, h)),
        out_shape=jax.ShapeDtypeStruct((batch * lp, A_QK), F32),
        scratch_shapes=[pltpu.VMEM((1, 2 * w), F32), pltpu.VMEM((VT_ROWS, 2 * w), F32),
                        pltpu.VMEM((w, 2 * w), F32), pltpu.VMEM((1, 2 * w), F32),
                        pltpu.VMEM((w, 2 * w), F32), pltpu.VMEM((1, 2 * w), F32)],
        compiler_params=_cparams(("parallel", "parallel", "arbitrary")),
        name="flash_prompt",
    )(jnp.asarray(coefs), lam_rows, q, ka, vt)


def _decode_kernel(pt_ref, lam_ref, q_ref, kn_ref, vn_ref, bias_ref, slope_ref, *rest, lam_init, past_len):
    del pt_ref
    k_refs = rest[:DEC_PAGES]
    v_refs = rest[DEC_PAGES:2 * DEC_PAGES]
    o_ref, m_sc, l_sc, acc_sc = rest[2 * DEC_PAGES:]
    j = pl.program_id(1)
    qq = _split_maps(q_ref[...]).astype(BF16)
    slope_col = slope_ref[...]

    @pl.when(j == 0)
    def _():
        kn = kn_ref[...]
        kn2 = jnp.concatenate([kn, kn], axis=0)
        m_sc[...] = jnp.sum(qq.astype(F32) * kn2, axis=-1, keepdims=True)
        l_sc[...] = jnp.ones_like(l_sc)
        vn = vn_ref[...]
        acc_sc[...] = jnp.concatenate([vn, vn], axis=0)

    bias = bias_ref[...]
    width = PAGE_SIZE * A_HEADS
    s_pages = []
    for i in range(DEC_PAGES):
        page = j * DEC_PAGES + i
        k = k_refs[i][...].reshape(width, A_HD2).astype(BF16)
        offset = slope_col * (page * PAGE_SIZE - past_len).astype(F32)
        s_pages.append(lax.dot_general(qq, k, NT_DIMS, preferred_element_type=F32) + (bias + offset))
    m_old = m_sc[...]
    m_new = m_old
    for s in s_pages:
        m_new = jnp.maximum(m_new, jnp.max(s, axis=-1, keepdims=True))
    alpha = jnp.exp2(m_old - m_new)
    l_new = alpha * l_sc[...]
    acc = alpha * acc_sc[...]
    for i, s in enumerate(s_pages):
        p = jnp.exp2(s - m_new)
        l_new += jnp.sum(p, axis=-1, keepdims=True)
        v = v_refs[i][...].reshape(width, A_HD2).astype(BF16)
        acc += jnp.dot(p.astype(BF16), v, preferred_element_type=F32)
    l_sc[...] = l_new
    acc_sc[...] = acc
    m_sc[...] = m_new

    @pl.when(j == pl.num_programs(1) - 1)
    def _():
        lam = _lambda_value(lam_ref, lam_init)
        o = acc_sc[...] / l_sc[...]
        o_ref[...] = o[:A_HEADS] - lam * o[A_HEADS:]


def decode_attn(q_s, k_s, v_s, cache_k, cache_v, page_table, layer, lam_rows, lam_init):
    db = q_s.shape[0]
    n_pages = page_table.shape[1]
    past_len = n_pages * PAGE_SIZE
    slopes = np.array([2.0 ** (-8.0 * (i + 1) / A_HEADS) * LOG2E for i in range(A_HEADS)], np.float32)
    rows_h = np.tile(np.arange(A_HEADS), 2)
    cols_t = np.repeat(np.arange(PAGE_SIZE), A_HEADS)
    cols_h = np.tile(np.arange(A_HEADS), PAGE_SIZE)
    bias = np.where(rows_h[:, None] == cols_h[None, :],
                    slopes[rows_h][:, None] * cols_t[None, :].astype(np.float32),
                    np.float32(MASK_VALUE)).astype(np.float32)
    slope_col = slopes[rows_h][:, None]

    hd = lambda a: a.reshape(db, A_HEADS, A_HD2)
    per_b = pl.BlockSpec((None, A_HEADS, A_HD2), lambda b, j, pt: (b, 0, 0))
    const2 = lambda b, j, pt: (0, 0)

    def page_spec(i):
        return pl.BlockSpec((None, None, PAGE_SIZE, A_HEADS, A_HD2),
                            lambda b, j, pt: (pt[b, j * DEC_PAGES + i], layer, 0, 0, 0))

    grid_spec = pltpu.PrefetchScalarGridSpec(
        num_scalar_prefetch=1,
        grid=(db, n_pages // DEC_PAGES),
        in_specs=[pl.BlockSpec((4, A_HEAD_DIM), const2), per_b, per_b, per_b,
                  pl.BlockSpec((2 * A_HEADS, PAGE_SIZE * A_HEADS), const2),
                  pl.BlockSpec((2 * A_HEADS, 1), const2)]
                 + [page_spec(i) for i in range(DEC_PAGES)] * 2,
        out_specs=per_b,
        scratch_shapes=[pltpu.VMEM((2 * A_HEADS, 1), F32), pltpu.VMEM((2 * A_HEADS, 1), F32),
                        pltpu.VMEM((2 * A_HEADS, A_HD2), F32)],
    )
    out = pl.pallas_call(
        functools.partial(_decode_kernel, lam_init=lam_init, past_len=past_len),
        grid_spec=grid_spec,
        out_shape=jax.ShapeDtypeStruct((db, A_HEADS, A_HD2), F32),
        compiler_params=_cparams(("parallel", "arbitrary")),
        name="decode_attn",
    )(page_table, lam_rows, hd(q_s.astype(F32)), hd(k_s), hd(v_s), jnp.asarray(bias), jnp.asarray(slope_col),
      *([cache_k] * DEC_PAGES), *([cache_v] * DEC_PAGES))
    return out.reshape(db, A_QK)


G_LEVELS = tuple(G_CHUNK >> (i + 1) for i in range(G_CHUNK.bit_length() - 1))


def _gla_decay_matrix():
    c = G_CHUNK
    r = np.arange(c)[:, None]
    j = np.arange(c)[None, :]
    blocks = [j <= r, j > r]
    for blk in G_LEVELS:
        ref = (r // (2 * blk)) * (2 * blk) + blk
        blocks.append(np.where(r >= ref, (j > ref) & (j <= r), (j > r) & (j <= ref)))
    return np.concatenate(blocks, axis=0).astype(np.float32)


def _gla_level_masks():
    c = G_CHUNK
    n = G_HEADS * c
    row = lax.broadcasted_iota(jnp.int32, (c, 1), 0)
    t_i = lax.broadcasted_iota(jnp.int32, (n, n), 0)
    s_i = lax.broadcasted_iota(jnp.int32, (n, n), 1)
    later, owns = [], []
    for blk in G_LEVELS:
        shift = blk.bit_length()
        later.append((row & blk) != 0)
        owns.append(((t_i >> shift) == (s_i >> shift)) & ((t_i & blk) != 0) & ((s_i & blk) == 0))
    return later, owns, t_i == s_i


def _heads_to_rows(x, width):
    return jnp.concatenate([x[:, h * width:(h + 1) * width] for h in range(G_HEADS)], axis=0)


def _gla_chunk(q, k, v, la, states, dmat, masks):
    c = q.shape[0]
    later, owns, diagonal = masks
    l1 = la.astype(BF16)
    l2 = (la - l1.astype(F32)).astype(BF16)
    e = jnp.dot(dmat, l1, preferred_element_type=F32) + jnp.dot(dmat, l2, preferred_element_type=F32)
    b = e[0:c]
    vb = v.astype(BF16)

    qe = (q * jnp.exp2(b)).astype(BF16)
    qk = _heads_to_rows(q * k, G_DK)
    a = jnp.where(diagonal, jnp.sum(qk, axis=-1, keepdims=True), 0.0)
    for i in range(len(G_LEVELS)):
        x = (jnp.where(later[i], q, k) * jnp.exp2(e[(2 + i) * c:(3 + i) * c])).astype(BF16)
        xs = _heads_to_rows(x, G_DK)
        a = jnp.where(owns[i], lax.dot_general(xs, xs, NT_DIMS, preferred_element_type=F32), a)
    o_intra = jnp.dot(a.astype(BF16), _heads_to_rows(vb, G_DV), preferred_element_type=F32)

    k_dec = (k * jnp.exp2(e[c:2 * c])).astype(BF16)
    decay = jnp.exp2(b[c - 1:c])
    outs, new_states = [], []
    for h in range(G_HEADS):
        ks = slice(h * G_DK, (h + 1) * G_DK)
        st = states[h]
        o_inter = lax.dot_general(qe[:, ks], st.astype(BF16), NT_DIMS, preferred_element_type=F32)
        outs.append(o_inter + o_intra[h * c:(h + 1) * c])
        new_states.append(st * decay[:, ks] + lax.dot_general(
            vb[:, h * G_DV:(h + 1) * G_DV], k_dec[:, ks], TN_DIMS, preferred_element_type=F32))
    return outs, new_states


def _gla_kernel(dmat_ref, q_ref, k_ref, v_ref, la_ref, o_ref, s_ref, st_sc, *, valid_len):
    i = pl.program_id(1)

    @pl.when(i == 0)
    def _():
        st_sc[...] = jnp.zeros_like(st_sc)

    dmat = dmat_ref[...]
    masks = _gla_level_masks()
    states = [st_sc[h] for h in range(G_HEADS)]
    for c in range(GLA_ROWS // G_CHUNK):
        sl = slice(c * G_CHUNK, (c + 1) * G_CHUNK)
        pos = i * GLA_ROWS + c * G_CHUNK + lax.broadcasted_iota(jnp.int32, (G_CHUNK, 1), 0)
        valid = pos < valid_len
        la = jnp.where(valid, la_ref[sl, :], 0.0)
        k = jnp.where(valid, k_ref[sl, :], 0.0)
        outs, states = _gla_chunk(q_ref[sl, :], k, v_ref[sl, :], la, states, dmat, masks)
        for h in range(G_HEADS):
            o_ref[sl, h * G_DV:(h + 1) * G_DV] = outs[h]
    for h in range(G_HEADS):
        st_sc[h] = states[h]

    @pl.when(i == pl.num_programs(1) - 1)
    def _():
        for h in range(G_HEADS):
            s_ref[h] = states[h].T


def gla_prompt(q, k, v, la, batch, lp, valid_len):
    nblk = lp // GLA_ROWS
    qk_spec = pl.BlockSpec((GLA_ROWS, G_QK), lambda b, i: (b * nblk + i, 0))
    v_spec = pl.BlockSpec((GLA_ROWS, G_V), lambda b, i: (b * nblk + i, 0))
    dmat = jnp.asarray(_gla_decay_matrix(), BF16)
    return pl.pallas_call(
        functools.partial(_gla_kernel, valid_len=valid_len),
        grid=(batch, nblk),
        in_specs=[pl.BlockSpec(dmat.shape, lambda b, i: (0, 0)), qk_spec, qk_spec, v_spec, qk_spec],
        out_specs=[v_spec, pl.BlockSpec((None, G_HEADS, G_DK, G_DV), lambda b, i: (b, 0, 0, 0))],
        out_shape=[jax.ShapeDtypeStruct((batch * lp, G_V), F32),
                   jax.ShapeDtypeStruct((batch, G_HEADS, G_DK, G_DV), F32)],
        scratch_shapes=[pltpu.VMEM((G_HEADS, G_DV, G_DK), F32)],
        compiler_params=_cparams(("parallel", "arbitrary")),
        name="gla_prompt",
    )(dmat, q, k, v, la)


def _stack_rows(rows, n_rows=16):
    n = rows[0].shape[1]
    idx = lax.broadcasted_iota(jnp.int32, (n_rows, n), 0)
    out = jnp.zeros((n_rows, n), F32)
    for r, x in enumerate(rows):
        out = jnp.where(idx == r, x, out)
    return out.astype(BF16)


def _gla_step_kernel(q_ref, k_ref, v_ref, la_ref, s_ref, o_ref, so_ref):
    q = q_ref[...]
    k = k_ref[...]
    v = v_ref[...]
    a = jnp.exp2(la_ref[...])
    ones = jnp.ones((16, G_DV), BF16)
    o_parts = []
    for h in range(G_HEADS):
        ks = slice(h * G_DK, (h + 1) * G_DK)
        vs = slice(h * G_DV, (h + 1) * G_DV)
        a_h = a[:, ks]
        a1 = a_h.astype(BF16)
        r1 = a_h - a1.astype(F32)
        a2 = r1.astype(BF16)
        a3 = (r1 - a2.astype(F32)).astype(BF16)
        a_rows = _stack_rows([a1.astype(F32), a2.astype(F32), a3.astype(F32)])
        a_col = lax.dot_general(a_rows, ones, TN_DIMS, preferred_element_type=F32)
        kv = lax.dot_general(_stack_rows([k[:, ks]]), _stack_rows([v[:, vs]]), TN_DIMS,
                             preferred_element_type=F32)
        s1 = a_col * s_ref[h] + kv
        so_ref[h] = s1
        o = jnp.dot(_stack_rows([q[:, ks]]), s1.astype(BF16), preferred_element_type=F32)
        o_parts.append(o[0:1])
    o_ref[...] = jnp.concatenate(o_parts, axis=1)


def gla_step(q_s, k_s, v_s, la_s, state_gla, layer):
    db = q_s.shape[0]
    r3 = lambda a: a.reshape(db, 1, a.shape[1])
    vec = lambda n: pl.BlockSpec((None, 1, n), lambda b: (b, 0, 0))
    o, s_new = pl.pallas_call(
        _gla_step_kernel,
        grid=(db,),
        in_specs=[vec(G_QK), vec(G_QK), vec(G_V), vec(G_QK),
                  pl.BlockSpec((None, None, G_HEADS, G_DK, G_DV), lambda b: (b, layer, 0, 0, 0))],
        out_specs=[vec(G_V), pl.BlockSpec((None, G_HEADS, G_DK, G_DV), lambda b: (b, 0, 0, 0))],
        out_shape=[jax.ShapeDtypeStruct((db, 1, G_V), F32),
                   jax.ShapeDtypeStruct((db, G_HEADS, G_DK, G_DV), F32)],
        compiler_params=_cparams(("parallel",)),
        name="gla_step",
    )(r3(q_s), r3(k_s), r3(v_s), r3(la_s), state_gla)
    return o.reshape(db, G_V), s_new


def kernel(x_prompt, x_sample, cache_k, cache_v, state_gla, page_table, meta_tokens, attn_w_in, attn_lq1, attn_lk1, attn_lq2, attn_lk2, attn_subln_w, attn_w_out, gla_w_in, gla_w_gate_up, gla_b_gate, gla_norm_w, gla_w_out, ln_g, ln_b):
    batch, seq, _ = x_prompt.shape
    db = x_sample.shape[0]
    length = N_META + seq
    lp = -(-length // ATT_Q) * ATT_Q
    meta = jnp.broadcast_to(meta_tokens[None].astype(x_prompt.dtype), (batch, N_META, D_MODEL))
    h_p = lax.pad(x_prompt, jnp.zeros((), x_prompt.dtype), ((0, 0, 0), (N_META, lp - length, 0), (0, 0, 0)))
    h_p = lax.dynamic_update_slice(h_p, meta, (0, 0, 0)).reshape(batch * lp, D_MODEL)
    h_s = x_sample.reshape(db, D_MODEL)

    k_p_rows, v_p_rows, k_s_rows, v_s_rows, s_p_list, s_s_list = [], [], [], [], [], []
    for i in range(DEPTH):
        if i % 2 == 0:
            a = i // 2
            lam_init = 0.8 - 0.6 * math.exp(-0.3 * i)
            w_in = attn_w_in[a].astype(BF16)
            w_out = attn_w_out[a].astype(BF16)
            lam_rows = jnp.stack([attn_lq1[a], attn_lk1[a], attn_lq2[a], attn_lk2[a]]).astype(F32)
            q_p, k_p, v_p, g_p, ka_p, vt_p = attn_proj(h_p, w_in, ATT_TILE, batch, length)
            q_s, k_s, v_s, g_s = attn_proj(h_s, w_in, db)
            o_p = flash_prompt(q_p, ka_p, vt_p, lam_rows, lam_init, batch, lp, length)
            o_s = decode_attn(q_s, k_s, v_s, cache_k, cache_v, page_table, a, lam_rows, lam_init)
            fin = functools.partial(finish, w_bf=w_out, norm_w=attn_subln_w[a], ln_g=ln_g[i], ln_b=ln_b[i],
                                    head_dim=A_HD2, scale=1.0 - lam_init)
            h_p = fin(o_p, g_p, h_p, tm=PROJ_ROWS)
            h_s = fin(o_s, g_s, h_s, tm=db)
            kv_shape = (batch, length, A_HEADS, A_HD2)
            k_p_rows.append(k_p.reshape(kv_shape))
            v_p_rows.append(v_p.reshape(kv_shape))
            k_s_rows.append(k_s.reshape(db, 1, A_HEADS, A_HD2))
            v_s_rows.append(v_s.reshape(db, 1, A_HEADS, A_HD2))
        else:
            gi = i // 2
            extra = G_RANK_PAD - G_RANK
            n_main = 2 * G_QK + 2 * G_V
            w_in = gla_w_in[gi][:, :n_main].astype(BF16)
            w_rank = jnp.pad(gla_w_in[gi][:, n_main:], ((0, 0), (0, extra))).astype(BF16)
            w_up = jnp.pad(gla_w_gate_up[gi], ((0, extra), (0, 0))).astype(BF16)
            w_out = gla_w_out[gi].astype(BF16)
            b_gate = gla_b_gate[gi].reshape(1, G_QK).astype(F32)
            q_p, k_p, v_p, g_p, la_p = gla_proj(h_p, w_in, w_rank, w_up, b_gate, PROJ_ROWS)
            q_s, k_s, v_s, g_s, la_s = gla_proj(h_s, w_in, w_rank, w_up, b_gate, db)
            o_p, s_p = gla_prompt(q_p, k_p, v_p, la_p, batch, lp, length)
            o_s, s_s = gla_step(q_s, k_s, v_s, la_s, state_gla, gi)
            fin = functools.partial(finish, w_bf=w_out, norm_w=gla_norm_w[gi], ln_g=ln_g[i], ln_b=ln_b[i],
                                    head_dim=G_DV, scale=1.0)
            h_p = fin(o_p, g_p, h_p, tm=PROJ_ROWS)
            h_s = fin(o_s, g_s, h_s, tm=db)
            s_p_list.append(s_p)
            s_s_list.append(s_s)

    y_prompt = h_p.reshape(batch, lp, D_MODEL)[:, N_META:length]
    return (y_prompt, h_s.reshape(db, 1, D_MODEL),
            jnp.stack(k_p_rows, axis=1), jnp.stack(v_p_rows, axis=1), jnp.stack(s_p_list, axis=1),
            jnp.stack(k_s_rows, axis=1), jnp.stack(v_s_rows, axis=1), jnp.stack(s_s_list, axis=1))
```

```python
import functools
import math

import numpy as np
import jax
import jax.numpy as jnp
from jax import lax
from jax.experimental import pallas as pl
from jax.experimental.pallas import tpu as pltpu

F32 = jnp.float32
BF16 = jnp.bfloat16

D_MODEL = 1024
DEPTH = 4
N_META = 16
PAGE_SIZE = 128
A_HEADS = 8
A_HEAD_DIM = 64
A_HD2 = 2 * A_HEAD_DIM
A_QK = A_HEADS * A_HD2
G_HEADS = 4
G_DK = 128
G_DV = 256
G_QK = G_HEADS * G_DK
G_V = G_HEADS * G_DV
G_RANK = 16
G_RANK_PAD = 128
G_NORMALIZER = 16.0
G_CHUNK = 64
DN_ALPHA = (2.0 * DEPTH) ** 0.25
LN_EPS = 1e-5
RMS_EPS = 1e-5

MASK_VALUE = -1e30
LOG2E = math.log2(math.e)
ATT_TILE = 256
ATT_Q = 2 * ATT_TILE
VT_ROWS = A_HD2 + 16
GLA_ROWS = 128
PROJ_ROWS = 512
DEC_PAGES = 8
VMEM_LIMIT = 56 * 1024 * 1024

NT_DIMS = (((1,), (1,)), ((), ()))
TN_DIMS = (((0,), (0,)), ((), ()))


def _cparams(sem):
    return pltpu.CompilerParams(dimension_semantics=sem, vmem_limit_bytes=VMEM_LIMIT)


def _proj_tile(step, tiles_per_seq, last_kv_tile):
    return jnp.where(step >= last_kv_tile, tiles_per_seq - 1 + last_kv_tile - step, step)


def _attn_proj_kernel(x_ref, w_ref, *refs, tiles_per_seq, last_kv_tile, prev_kv):
    n_in = 2 if prev_kv else 0
    q_ref, k_ref, v_ref, g_ref = refs[n_in:n_in + 4]
    flash_refs = refs[:n_in] + refs[n_in + 4:]
    x = x_ref[...].astype(BF16)

    def cols(c):
        return jnp.dot(x, w_ref[:, c * A_QK:(c + 1) * A_QK], preferred_element_type=F32)

    q_ref[...] = (cols(0) * (A_HEAD_DIM ** -0.5 * LOG2E)).astype(BF16)
    k = cols(1)
    v = cols(2)
    g_ref[...] = cols(3)
    if tiles_per_seq is None:
        k_ref[...] = k
        v_ref[...] = v
        return
    if prev_kv:
        kprev_ref, vprev_ref, ka_ref, vt_ref = flash_refs
        k_ref[0] = kprev_ref[...]
        v_ref[0] = vprev_ref[...]
        k_ref[1] = k
        v_ref[1] = v
    else:
        ka_ref, vt_ref = flash_refs
        k_ref[...] = k
        v_ref[...] = v
    t = ATT_TILE
    tile = _proj_tile(pl.program_id(0) % tiles_per_seq, tiles_per_seq, last_kv_tile)
    pos_lo = lax.broadcasted_iota(jnp.int32, (t, A_HD2), 0).astype(F32)
    lane = lax.broadcasted_iota(jnp.int32, (t, A_HD2), 1)
    parity = (tile % 2).astype(F32)
    pos_cols = jnp.where(lane < 3, pos_lo, jnp.where(lane < 6, parity, 0.0)).astype(BF16)
    sub = lax.broadcasted_iota(jnp.int32, (VT_ROWS - A_HD2, t), 0)
    ones_rows = jnp.where(sub == 0, 1.0, 0.0).astype(BF16)
    for h in range(A_HEADS):
        hs = slice(h * A_HD2, (h + 1) * A_HD2)
        ka_ref[:, 2 * h * A_HD2:(2 * h + 1) * A_HD2] = k[:, hs].astype(BF16)
        ka_ref[:, (2 * h + 1) * A_HD2:(2 * h + 2) * A_HD2] = pos_cols
        vt_ref[h, 0:A_HD2, :] = v[:, hs].T.astype(BF16)
        vt_ref[h, A_HD2:VT_ROWS, :] = ones_rows


def attn_proj(x, w_bf, tm, batch=None, valid_len=None, prev_kv=()):
    m = x.shape[0]
    full = lambda i: (0, 0)
    in_specs = [None, pl.BlockSpec((D_MODEL, 4 * A_QK), full)]
    if batch is None:
        tiles_per_seq = last_kv_tile = None
        row = lambda i: (i, 0)
        blk = pl.BlockSpec((tm, A_QK), row)
        sds = lambda dt: jax.ShapeDtypeStruct((m, A_QK), dt)
        out_specs = [blk] * 4
        out_shape = [sds(BF16), sds(F32), sds(F32), sds(F32)]
    else:
        assert tm == ATT_TILE
        tiles_per_seq = m // batch // tm
        last_kv_tile = (valid_len - 1) // tm
        seq = lambda i: i // tiles_per_seq
        tile = lambda i: _proj_tile(i % tiles_per_seq, tiles_per_seq, last_kv_tile)
        row = lambda i: (seq(i) * tiles_per_seq + tile(i), 0)
        blk = pl.BlockSpec((tm, A_QK), row)
        sds = lambda dt: jax.ShapeDtypeStruct((m, A_QK), dt)
        kv_tile = lambda i: jnp.minimum(tile(i), last_kv_tile)
        kv_in = pl.BlockSpec((None, tm, A_QK), lambda i: (seq(i), kv_tile(i), 0))
        if prev_kv:
            kv_out = pl.BlockSpec((None, 2, tm, A_QK), lambda i: (seq(i), 0, kv_tile(i), 0))
            kv_sds = jax.ShapeDtypeStruct((batch, 2, valid_len, A_QK), F32)
            in_specs += [kv_in, kv_in]
        else:
            kv_out = kv_in
            kv_sds = jax.ShapeDtypeStruct((batch, valid_len, A_QK), F32)
        out_specs = [blk, kv_out, kv_out, blk,
                     pl.BlockSpec((tm, 2 * A_QK), row),
                     pl.BlockSpec((None, A_HEADS, None, VT_ROWS, tm), lambda i: (seq(i), 0, tile(i), 0, 0))]
        out_shape = [sds(BF16), kv_sds, kv_sds, sds(F32),
                     jax.ShapeDtypeStruct((m, 2 * A_QK), BF16),
                     jax.ShapeDtypeStruct((batch, A_HEADS, tiles_per_seq, VT_ROWS, tm), BF16)]
    in_specs[0] = pl.BlockSpec((tm, D_MODEL), row)
    return pl.pallas_call(
        functools.partial(_attn_proj_kernel, tiles_per_seq=tiles_per_seq, last_kv_tile=last_kv_tile,
                          prev_kv=bool(prev_kv)),
        grid=(m // tm,),
        in_specs=in_specs,
        out_specs=out_specs,
        out_shape=out_shape,
        compiler_params=_cparams(("arbitrary",)),
        name="attn_proj",
    )(x, w_bf, *prev_kv)


def _log_sigmoid(x):
    return jnp.minimum(x, 0.0) - jnp.log1p(jnp.exp(-jnp.abs(x)))


def _gla_proj_kernel(x_ref, w_ref, wr_ref, wup_ref, bg_ref, q_ref, k_ref, v_ref, g_ref, la_ref):
    x = x_ref[...].astype(BF16)

    def cols(a, b):
        return jnp.dot(x, w_ref[:, a:b], preferred_element_type=F32)

    q_ref[...] = cols(0, G_QK) * (G_DK ** -0.5)
    k_ref[...] = cols(G_QK, 2 * G_QK)
    v_ref[...] = cols(2 * G_QK, 2 * G_QK + G_V)
    g_ref[...] = cols(2 * G_QK + G_V, 2 * G_QK + 2 * G_V)
    r = jnp.dot(x, wr_ref[...], preferred_element_type=F32)
    gate = jnp.dot(r.astype(BF16), wup_ref[...], preferred_element_type=F32) + bg_ref[...]
    la_ref[...] = _log_sigmoid(gate) * (LOG2E / G_NORMALIZER)


def gla_proj(x, w_bf, wr_bf, wup_bf, b_gate, tm):
    m = x.shape[0]
    row = lambda i: (i, 0)
    full = lambda i: (0, 0)
    return pl.pallas_call(
        _gla_proj_kernel,
        grid=(m // tm,),
        in_specs=[pl.BlockSpec((tm, D_MODEL), row),
                  pl.BlockSpec((D_MODEL, 2 * G_QK + 2 * G_V), full),
                  pl.BlockSpec((D_MODEL, G_RANK_PAD), full),
                  pl.BlockSpec((G_RANK_PAD, G_QK), full),
                  pl.BlockSpec((1, G_QK), full)],
        out_specs=[pl.BlockSpec((tm, G_QK), row), pl.BlockSpec((tm, G_QK), row),
                   pl.BlockSpec((tm, G_V), row), pl.BlockSpec((tm, G_V), row),
                   pl.BlockSpec((tm, G_QK), row)],
        out_shape=[jax.ShapeDtypeStruct((m, G_QK), F32), jax.ShapeDtypeStruct((m, G_QK), F32),
                   jax.ShapeDtypeStruct((m, G_V), F32), jax.ShapeDtypeStruct((m, G_V), F32),
                   jax.ShapeDtypeStruct((m, G_QK), F32)],
        compiler_params=_cparams(("parallel",)),
        name="gla_proj",
    )(x, w_bf, wr_bf, wup_bf, b_gate)


def _finish_kernel(o_ref, g_ref, h_ref, w_ref, nw_ref, lg_ref, lb_ref, out_ref, *, head_dim, scale):
    o = o_ref[...]
    nw = nw_ref[...]
    parts = []
    for c in range(o.shape[1] // head_dim):
        oh = o[:, c * head_dim:(c + 1) * head_dim]
        ms = jnp.mean(oh * oh, axis=-1, keepdims=True)
        parts.append(oh * lax.rsqrt(ms + RMS_EPS) * nw)
    on = jnp.concatenate(parts, axis=1)
    if scale != 1.0:
        on = on * scale
    g = g_ref[...]
    y = on * (g * jax.nn.sigmoid(g))
    y = jnp.dot(y.astype(BF16), w_ref[...], preferred_element_type=F32)
    x = DN_ALPHA * h_ref[...] + y
    mu = jnp.mean(x, axis=-1, keepdims=True)
    xc = x - mu
    var = jnp.mean(xc * xc, axis=-1, keepdims=True)
    out_ref[...] = xc * lax.rsqrt(var + LN_EPS) * lg_ref[...] + lb_ref[...]


def finish(o, g, h, w_bf, norm_w, ln_g, ln_b, head_dim, scale, tm):
    m, n = o.shape
    row = lambda i: (i, 0)
    full = lambda i: (0, 0)
    return pl.pallas_call(
        functools.partial(_finish_kernel, head_dim=head_dim, scale=scale),
        grid=(m // tm,),
        in_specs=[pl.BlockSpec((tm, n), row), pl.BlockSpec((tm, n), row),
                  pl.BlockSpec((tm, D_MODEL), row),
                  pl.BlockSpec((n, D_MODEL), full),
                  pl.BlockSpec((1, head_dim), full),
                  pl.BlockSpec((1, D_MODEL), full), pl.BlockSpec((1, D_MODEL), full)],
        out_specs=pl.BlockSpec((tm, D_MODEL), row),
        out_shape=jax.ShapeDtypeStruct((m, D_MODEL), F32),
        compiler_params=_cparams(("parallel",)),
        name="finish",
    )(o, g, h, w_bf, norm_w.reshape(1, head_dim), ln_g.reshape(1, D_MODEL), ln_b.reshape(1, D_MODEL))


def _lambda_value(lam_ref, lam_init):
    lv = lam_ref[...]
    e1 = jnp.exp(jnp.sum(lv[0:1] * lv[1:2], axis=-1, keepdims=True))
    e2 = jnp.exp(jnp.sum(lv[2:3] * lv[3:4], axis=-1, keepdims=True))
    return e1 - e2 + lam_init


def _split_maps(q):
    lane = lax.broadcasted_iota(jnp.int32, q.shape, 1)
    zero = jnp.zeros_like(q)
    return jnp.concatenate([jnp.where(lane < A_HEAD_DIM, q, zero),
                            jnp.where(lane >= A_HEAD_DIM, q, zero)], axis=0)


def _bf16_terms(x):
    t1 = x.astype(BF16).astype(F32)
    r1 = x - t1
    t2 = r1.astype(BF16).astype(F32)
    t3 = (r1 - t2).astype(BF16).astype(F32)
    return t1, t2, t3


def _flash_kernel(c_ref, lam_ref, q_ref, ka_ref, vt_ref, o_ref, m_sc, acc_sc, sa_sc, ma_sc, sb_sc, mb_sc,
                  *, lam_init, last_rows):
    t = ATT_TILE
    w = ATT_Q
    h = pl.program_id(1)
    qi = pl.program_id(2)
    c = c_ref[h]
    lane = lax.broadcasted_iota(jnp.int32, (1, A_HD2), 1)
    coef = jnp.where(lane < 3, c, jnp.where(lane < 6, c * t, 0.0))
    t1, t2, t3 = _bf16_terms(coef)
    part0 = (lane == 0) | (lane == 3)
    part1 = (lane == 1) | (lane == 4)
    coef = jnp.where(part0, t1, jnp.where(part1, t2, t3)).astype(BF16)

    def attend(nq_rows):
        cols = 2 * nq_rows
        qa = jnp.concatenate([_split_maps(q_ref[0:nq_rows, :]),
                              jnp.broadcast_to(coef, (cols, A_HD2))], axis=1)

        def scores(u):
            start = pl.multiple_of(u * w, w)
            return lax.dot_general(ka_ref[pl.ds(start, w), :], qa, NT_DIMS,
                                   preferred_element_type=F32)

        def prefetch(u, s_ref, smax_ref):
            s = scores(u)
            s_ref[:, 0:cols] = s
            smax_ref[:, 0:cols] = jnp.max(s, axis=0, keepdims=True)

        def accumulate(s, s_max, u):
            offset = c * (w * (u - qi)).astype(F32)
            m_old = m_sc[:, 0:cols]
            m_rel = jnp.maximum(m_old - offset, s_max)
            m_new = m_rel + offset
            alpha = jnp.exp2(m_old - m_new)
            p = jnp.exp2(s - m_rel).astype(BF16)
            pv = jnp.dot(vt_ref[2 * u], p[0:t], preferred_element_type=F32)
            pv += jnp.dot(vt_ref[2 * u + 1], p[t:w], preferred_element_type=F32)
            acc_sc[:, 0:cols] = alpha * acc_sc[:, 0:cols] + pv
            m_sc[:, 0:cols] = m_new

        def accumulate_diagonal(s):
            key = lax.broadcasted_iota(jnp.int32, (w, cols), 0)
            col = lax.broadcasted_iota(jnp.int32, (w, cols), 1)
            s = jnp.where(key <= jnp.where(col >= nq_rows, col - nq_rows, col), s, MASK_VALUE)
            accumulate(s, jnp.max(s, axis=0, keepdims=True), qi)

        m_sc[...] = jnp.full_like(m_sc, MASK_VALUE)
        acc_sc[...] = jnp.zeros_like(acc_sc)
        prefetch(0, sa_sc, ma_sc)

        def body(jj, carry):
            u = 2 * jj
            prefetch(u + 1, sb_sc, mb_sc)
            accumulate(sa_sc[:, 0:cols], ma_sc[:, 0:cols], u)
            prefetch(u + 2, sa_sc, ma_sc)
            accumulate(sb_sc[:, 0:cols], mb_sc[:, 0:cols], u + 1)
            return carry

        lax.fori_loop(0, qi // 2, body, 0)

        @pl.when(qi % 2 == 1)
        def _():
            s_diag = scores(qi)
            accumulate(sa_sc[:, 0:cols], ma_sc[:, 0:cols], qi - 1)
            accumulate_diagonal(s_diag)

        @pl.when(qi % 2 == 0)
        def _():
            accumulate_diagonal(sa_sc[:, 0:cols])

        lam = _lambda_value(lam_ref, lam_init)
        acc = acc_sc[:, 0:cols]
        o = acc[0:A_HD2] / acc[A_HD2:A_HD2 + 1]
        o_ref[0:nq_rows, :] = (o[:, :nq_rows] - lam * o[:, nq_rows:]).T
        if nq_rows < w:
            o_ref[nq_rows:w, :] = jnp.zeros((w - nq_rows, A_HD2), F32)

    if last_rows == w:
        attend(w)
    else:
        is_last = qi == pl.num_programs(2) - 1
        pl.when(jnp.logical_not(is_last))(lambda: attend(w))
        pl.when(is_last)(lambda: attend(last_rows))


def flash_prompt(q, ka, vt, lam_rows, lam_init, batch, lp, valid_len):
    w = ATT_Q
    nq = lp // w
    lane_tile = A_HD2
    last_rows = min(w, -(-(valid_len - (nq - 1) * w) // lane_tile) * lane_tile)
    coefs = np.array([2.0 ** (-8.0 * (i + 1) / A_HEADS) * LOG2E for i in range(A_HEADS)], np.float32)
    return pl.pallas_call(
        functools.partial(_flash_kernel, lam_init=lam_init, last_rows=last_rows),
        grid=(batch, A_HEADS, nq),
        in_specs=[pl.BlockSpec(memory_space=pltpu.SMEM),
                  pl.BlockSpec((4, A_HEAD_DIM), lambda b, h, i: (0, 0)),
                  pl.BlockSpec((w, A_HD2), lambda b, h, i: (b * nq + i, h)),
                  pl.BlockSpec((lp, 2 * A_HD2), lambda b, h, i: (b, h)),
                  pl.BlockSpec((None, None, lp // ATT_TILE, VT_ROWS, ATT_TILE),
                               lambda b, h, i: (b, h, 0, 0, 0))],
        out_specs=pl.BlockSpec((w, A_HD2), lambda b, h, i: (b * nq + i, h)),
        out_shape=jax.ShapeDtypeStruct((batch * lp, A_QK), F32),
        scratch_shapes=[pltpu.VMEM((1, 2 * w), F32), pltpu.VMEM((VT_ROWS, 2 * w), F32),
                        pltpu.VMEM((w, 2 * w), F32), pltpu.VMEM((1, 2 * w), F32),
                        pltpu.VMEM((w, 2 * w), F32), pltpu.VMEM((1, 2 * w), F32)],
        compiler_params=_cparams(("parallel", "parallel", "arbitrary")),
        name="flash_prompt",
    )(jnp.asarray(coefs), lam_rows, q, ka, vt)


def _decode_kernel(pt_ref, lam_ref, q_ref, kn_ref, vn_ref, bias_ref, slope_ref, *rest, lam_init, past_len):
    del pt_ref
    k_refs = rest[:DEC_PAGES]
    v_refs = rest[DEC_PAGES:2 * DEC_PAGES]
    o_ref, m_sc, l_sc, acc_sc = rest[2 * DEC_PAGES:]
    j = pl.program_id(1)
    qq = _split_maps(q_ref[...]).astype(BF16)
    slope_col = slope_ref[...]

    @pl.when(j == 0)
    def _():
        kn = kn_ref[...]
        kn2 = jnp.concatenate([kn, kn], axis=0)
        m_sc[...] = jnp.sum(qq.astype(F32) * kn2, axis=-1, keepdims=True)
        l_sc[...] = jnp.ones_like(l_sc)
        vn = vn_ref[...]
        acc_sc[...] = jnp.concatenate([vn, vn], axis=0)

    bias = bias_ref[...]
    width = PAGE_SIZE * A_HEADS
    s_pages = []
    for i in range(DEC_PAGES):
        page = j * DEC_PAGES + i
        k = k_refs[i][...].reshape(width, A_HD2).astype(BF16)
        offset = slope_col * (page * PAGE_SIZE - past_len).astype(F32)
        s_pages.append(lax.dot_general(qq, k, NT_DIMS, preferred_element_type=F32) + (bias + offset))
    m_old = m_sc[...]
    m_new = m_old
    for s in s_pages:
        m_new = jnp.maximum(m_new, jnp.max(s, axis=-1, keepdims=True))
    alpha = jnp.exp2(m_old - m_new)
    l_new = alpha * l_sc[...]
    acc = alpha * acc_sc[...]
    for i, s in enumerate(s_pages):
        p = jnp.exp2(s - m_new)
        l_new += jnp.sum(p, axis=-1, keepdims=True)
        v = v_refs[i][...].reshape(width, A_HD2).astype(BF16)
        acc += jnp.dot(p.astype(BF16), v, preferred_element_type=F32)
    l_sc[...] = l_new
    acc_sc[...] = acc
    m_sc[...] = m_new

    @pl.when(j == pl.num_programs(1) - 1)
    def _():
        lam = _lambda_value(lam_ref, lam_init)
        o = acc_sc[...] / l_sc[...]
        o_ref[...] = o[:A_HEADS] - lam * o[A_HEADS:]


def decode_attn(q_s, k_s, v_s, cache_k, cache_v, page_table, layer, lam_rows, lam_init):
    db = q_s.shape[0]
    n_pages = page_table.shape[1]
    past_len = n_pages * PAGE_SIZE
    slopes = np.array([2.0 ** (-8.0 * (i + 1) / A_HEADS) * LOG2E for i in range(A_HEADS)], np.float32)
    rows_h = np.tile(np.arange(A_HEADS), 2)
    cols_t = np.repeat(np.arange(PAGE_SIZE), A_HEADS)
    cols_h = np.tile(np.arange(A_HEADS), PAGE_SIZE)
    bias = np.where(rows_h[:, None] == cols_h[None, :],
                    slopes[rows_h][:, None] * cols_t[None, :].astype(np.float32),
                    np.float32(MASK_VALUE)).astype(np.float32)
    slope_col = slopes[rows_h][:, None]

    hd = lambda a: a.reshape(db, A_HEADS, A_HD2)
    per_b = pl.BlockSpec((None, A_HEADS, A_HD2), lambda b, j, pt: (b, 0, 0))
    const2 = lambda b, j, pt: (0, 0)

    def page_spec(i):
        return pl.BlockSpec((None, None, PAGE_SIZE, A_HEADS, A_HD2),
                            lambda b, j, pt: (pt[b, j * DEC_PAGES + i], layer, 0, 0, 0))

    grid_spec = pltpu.PrefetchScalarGridSpec(
        num_scalar_prefetch=1,
        grid=(db, n_pages // DEC_PAGES),
        in_specs=[pl.BlockSpec((4, A_HEAD_DIM), const2), per_b, per_b, per_b,
                  pl.BlockSpec((2 * A_HEADS, PAGE_SIZE * A_HEADS), const2),
                  pl.BlockSpec((2 * A_HEADS, 1), const2)]
                 + [page_spec(i) for i in range(DEC_PAGES)] * 2,
        out_specs=per_b,
        scratch_shapes=[pltpu.VMEM((2 * A_HEADS, 1), F32), pltpu.VMEM((2 * A_HEADS, 1), F32),
                        pltpu.VMEM((2 * A_HEADS, A_HD2), F32)],
    )
    out = pl.pallas_call(
        functools.partial(_decode_kernel, lam_init=lam_init, past_len=past_len),
        grid_spec=grid_spec,
        out_shape=jax.ShapeDtypeStruct((db, A_HEADS, A_HD2), F32),
        compiler_params=_cparams(("parallel", "arbitrary")),
        name="decode_attn",
    )(page_table, lam_rows, hd(q_s.astype(F32)), hd(k_s), hd(v_s), jnp.asarray(bias), jnp.asarray(slope_col),
      *([cache_k] * DEC_PAGES), *([cache_v] * DEC_PAGES))
    return out.reshape(db, A_QK)


G_LEVELS = tuple(G_CHUNK >> (i + 1) for i in range(G_CHUNK.bit_length() - 1))


def _gla_decay_matrix():
    c = G_CHUNK
    r = np.arange(c)[:, None]
    j = np.arange(c)[None, :]
    blocks = [j <= r, j > r]
    for blk in G_LEVELS:
        ref = (r // (2 * blk)) * (2 * blk) + blk
        blocks.append(np.where(r >= ref, (j > ref) & (j <= r), (j > r) & (j <= ref)))
    return np.concatenate(blocks, axis=0).astype(np.float32)


def _gla_level_masks():
    c = G_CHUNK
    n = G_HEADS * c
    row = lax.broadcasted_iota(jnp.int32, (c, 1), 0)
    t_i = lax.broadcasted_iota(jnp.int32, (n, n), 0)
    s_i = lax.broadcasted_iota(jnp.int32, (n, n), 1)
    later, owns = [], []
    for blk in G_LEVELS:
        shift = blk.bit_length()
        later.append((row & blk) != 0)
        owns.append(((t_i >> shift) == (s_i >> shift)) & ((t_i & blk) != 0) & ((s_i & blk) == 0))
    return later, owns, t_i == s_i


def _heads_to_rows(x, width):
    return jnp.concatenate([x[:, h * width:(h + 1) * width] for h in range(G_HEADS)], axis=0)


def _gla_chunk(q, k, v, la, states, dmat, masks):
    c = q.shape[0]
    later, owns, diagonal = masks
    l1 = la.astype(BF16)
    l2 = (la - l1.astype(F32)).astype(BF16)
    e = jnp.dot(dmat, l1, preferred_element_type=F32) + jnp.dot(dmat, l2, preferred_element_type=F32)
    b = e[0:c]
    vb = v.astype(BF16)

    qe = (q * jnp.exp2(b)).astype(BF16)
    qk = _heads_to_rows(q * k, G_DK)
    a = jnp.where(diagonal, jnp.sum(qk, axis=-1, keepdims=True), 0.0)
    for i in range(len(G_LEVELS)):
        x = (jnp.where(later[i], q, k) * jnp.exp2(e[(2 + i) * c:(3 + i) * c])).astype(BF16)
        xs = _heads_to_rows(x, G_DK)
        a = jnp.where(owns[i], lax.dot_general(xs, xs, NT_DIMS, preferred_element_type=F32), a)
    o_intra = jnp.dot(a.astype(BF16), _heads_to_rows(vb, G_DV), preferred_element_type=F32)

    k_dec = (k * jnp.exp2(e[c:2 * c])).astype(BF16)
    decay = jnp.exp2(b[c - 1:c])
    outs, new_states = [], []
    for h in range(G_HEADS):
        ks = slice(h * G_DK, (h + 1) * G_DK)
        st = states[h]
        o_inter = lax.dot_general(qe[:, ks], st.astype(BF16), NT_DIMS, preferred_element_type=F32)
        outs.append(o_inter + o_intra[h * c:(h + 1) * c])
        new_states.append(st * decay[:, ks] + lax.dot_general(
            vb[:, h * G_DV:(h + 1) * G_DV], k_dec[:, ks], TN_DIMS, preferred_element_type=F32))
    return outs, new_states


def _gla_kernel(dmat_ref, q_ref, k_ref, v_ref, la_ref, o_ref, s_ref, st_sc, *, valid_len):
    i = pl.program_id(1)

    @pl.when(i == 0)
    def _():
        st_sc[...] = jnp.zeros_like(st_sc)

    has_real_rows = i * GLA_ROWS < valid_len

    @pl.when(has_real_rows)
    def _():
        dmat = dmat_ref[...]
        masks = _gla_level_masks()
        states = [st_sc[h] for h in range(G_HEADS)]
        for c in range(GLA_ROWS // G_CHUNK):
            sl = slice(c * G_CHUNK, (c + 1) * G_CHUNK)
            pos = i * GLA_ROWS + c * G_CHUNK + lax.broadcasted_iota(jnp.int32, (G_CHUNK, 1), 0)
            valid = pos < valid_len
            la = jnp.where(valid, la_ref[sl, :], 0.0)
            k = jnp.where(valid, k_ref[sl, :], 0.0)
            outs, states = _gla_chunk(q_ref[sl, :], k, v_ref[sl, :], la, states, dmat, masks)
            for h in range(G_HEADS):
                o_ref[sl, h * G_DV:(h + 1) * G_DV] = outs[h]
        for h in range(G_HEADS):
            st_sc[h] = states[h]

    @pl.when(jnp.logical_not(has_real_rows))
    def _():
        o_ref[...] = jnp.zeros_like(o_ref)

    @pl.when(i == pl.num_programs(1) - 1)
    def _():
        for h in range(G_HEADS):
            s_ref[h] = st_sc[h].T


def gla_prompt(q, k, v, la, batch, lp, valid_len):
    nblk = lp // GLA_ROWS
    qk_spec = pl.BlockSpec((GLA_ROWS, G_QK), lambda b, i: (b * nblk + i, 0))
    v_spec = pl.BlockSpec((GLA_ROWS, G_V), lambda b, i: (b * nblk + i, 0))
    dmat = jnp.asarray(_gla_decay_matrix(), BF16)
    return pl.pallas_call(
        functools.partial(_gla_kernel, valid_len=valid_len),
        grid=(batch, nblk),
        in_specs=[pl.BlockSpec(dmat.shape, lambda b, i: (0, 0)), qk_spec, qk_spec, v_spec, qk_spec],
        out_specs=[v_spec, pl.BlockSpec((None, G_HEADS, G_DK, G_DV), lambda b, i: (b, 0, 0, 0))],
        out_shape=[jax.ShapeDtypeStruct((batch * lp, G_V), F32),
                   jax.ShapeDtypeStruct((batch, G_HEADS, G_DK, G_DV), F32)],
        scratch_shapes=[pltpu.VMEM((G_HEADS, G_DV, G_DK), F32)],
        compiler_params=_cparams(("parallel", "arbitrary")),
        name="gla_prompt",
    )(dmat, q, k, v, la)


def _stack_rows(rows, n_rows=16):
    n = rows[0].shape[1]
    idx = lax.broadcasted_iota(jnp.int32, (n_rows, n), 0)
    out = jnp.zeros((n_rows, n), F32)
    for r, x in enumerate(rows):
        out = jnp.where(idx == r, x, out)
    return out.astype(BF16)


def _gla_step_kernel(q_ref, k_ref, v_ref, la_ref, s_ref, o_ref, so_ref):
    q = q_ref[...]
    k = k_ref[...]
    v = v_ref[...]
    a = jnp.exp2(la_ref[...])
    ones = jnp.ones((16, G_DV), BF16)
    o_parts = []
    for h in range(G_HEADS):
        ks = slice(h * G_DK, (h + 1) * G_DK)
        vs = slice(h * G_DV, (h + 1) * G_DV)
        a_h = a[:, ks]
        a1 = a_h.astype(BF16)
        r1 = a_h - a1.astype(F32)
        a2 = r1.astype(BF16)
        a3 = (r1 - a2.astype(F32)).astype(BF16)
        a_rows = _stack_rows([a1.astype(F32), a2.astype(F32), a3.astype(F32)])
        a_col = lax.dot_general(a_rows, ones, TN_DIMS, preferred_element_type=F32)
        kv = lax.dot_general(_stack_rows([k[:, ks]]), _stack_rows([v[:, vs]]), TN_DIMS,
                             preferred_element_type=F32)
        s1 = a_col * s_ref[h] + kv
        so_ref[h] = s1
        o = jnp.dot(_stack_rows([q[:, ks]]), s1.astype(BF16), preferred_element_type=F32)
        o_parts.append(o[0:1])
    o_ref[...] = jnp.concatenate(o_parts, axis=1)


def gla_step(q_s, k_s, v_s, la_s, state_gla, layer):
    db = q_s.shape[0]
    r3 = lambda a: a.reshape(db, 1, a.shape[1])
    vec = lambda n: pl.BlockSpec((None, 1, n), lambda b: (b, 0, 0))
    o, s_new = pl.pallas_call(
        _gla_step_kernel,
        grid=(db,),
        in_specs=[vec(G_QK), vec(G_QK), vec(G_V), vec(G_QK),
                  pl.BlockSpec((None, None, G_HEADS, G_DK, G_DV), lambda b: (b, layer, 0, 0, 0))],
        out_specs=[vec(G_V), pl.BlockSpec((None, G_HEADS, G_DK, G_DV), lambda b: (b, 0, 0, 0))],
        out_shape=[jax.ShapeDtypeStruct((db, 1, G_V), F32),
                   jax.ShapeDtypeStruct((db, G_HEADS, G_DK, G_DV), F32)],
        compiler_params=_cparams(("parallel",)),
        name="gla_step",
    )(r3(q_s), r3(k_s), r3(v_s), r3(la_s), state_gla)
    return o.reshape(db, G_V), s_new


def kernel(x_prompt, x_sample, cache_k, cache_v, state_gla, page_table, meta_tokens, attn_w_in, attn_lq1, attn_lk1, attn_lq2, attn_lk2, attn_subln_w, attn_w_out, gla_w_in, gla_w_gate_up, gla_b_gate, gla_norm_w, gla_w_out, ln_g, ln_b):
    batch, seq, _ = x_prompt.shape
    db = x_sample.shape[0]
    length = N_META + seq
    lp = -(-length // ATT_Q) * ATT_Q
    meta = jnp.broadcast_to(meta_tokens[None].astype(x_prompt.dtype), (batch, N_META, D_MODEL))
    h_p = lax.pad(x_prompt, jnp.zeros((), x_prompt.dtype), ((0, 0, 0), (N_META, lp - length, 0), (0, 0, 0)))
    h_p = lax.dynamic_update_slice(h_p, meta, (0, 0, 0)).reshape(batch * lp, D_MODEL)
    h_s = x_sample.reshape(db, D_MODEL)

    assert DEPTH == 4
    kv_p = ()
    k_s_rows, v_s_rows, s_p_list, s_s_list = [], [], [], []
    for i in range(DEPTH):
        if i % 2 == 0:
            a = i // 2
            lam_init = 0.8 - 0.6 * math.exp(-0.3 * i)
            w_in = attn_w_in[a].astype(BF16)
            w_out = attn_w_out[a].astype(BF16)
            lam_rows = jnp.stack([attn_lq1[a], attn_lk1[a], attn_lq2[a], attn_lk2[a]]).astype(F32)
            q_p, k_p, v_p, g_p, ka_p, vt_p = attn_proj(h_p, w_in, ATT_TILE, batch, length, kv_p)
            kv_p = (k_p, v_p)
            q_s, k_s, v_s, g_s = attn_proj(h_s, w_in, db)
            o_p = flash_prompt(q_p, ka_p, vt_p, lam_rows, lam_init, batch, lp, length)
            o_s = decode_attn(q_s, k_s, v_s, cache_k, cache_v, page_table, a, lam_rows, lam_init)
            fin = functools.partial(finish, w_bf=w_out, norm_w=attn_subln_w[a], ln_g=ln_g[i], ln_b=ln_b[i],
                                    head_dim=A_HD2, scale=1.0 - lam_init)
            h_p = fin(o_p, g_p, h_p, tm=PROJ_ROWS)
            h_s = fin(o_s, g_s, h_s, tm=db)
            k_s_rows.append(k_s.reshape(db, 1, A_HEADS, A_HD2))
            v_s_rows.append(v_s.reshape(db, 1, A_HEADS, A_HD2))
        else:
            gi = i // 2
            extra = G_RANK_PAD - G_RANK
            n_main = 2 * G_QK + 2 * G_V
            w_in = gla_w_in[gi][:, :n_main].astype(BF16)
            w_rank = jnp.pad(gla_w_in[gi][:, n_main:], ((0, 0), (0, extra))).astype(BF16)
            w_up = jnp.pad(gla_w_gate_up[gi], ((0, extra), (0, 0))).astype(BF16)
            w_out = gla_w_out[gi].astype(BF16)
            b_gate = gla_b_gate[gi].reshape(1, G_QK).astype(F32)
            q_p, k_p, v_p, g_p, la_p = gla_proj(h_p, w_in, w_rank, w_up, b_gate, PROJ_ROWS)
            q_s, k_s, v_s, g_s, la_s = gla_proj(h_s, w_in, w_rank, w_up, b_gate, db)
            o_p, s_p = gla_prompt(q_p, k_p, v_p, la_p, batch, lp, length)
            o_s, s_s = gla_step(q_s, k_s, v_s, la_s, state_gla, gi)
            fin = functools.partial(finish, w_bf=w_out, norm_w=gla_norm_w[gi], ln_g=ln_g[i], ln_b=ln_b[i],
                                    head_dim=G_DV, scale=1.0)
            h_p = fin(o_p, g_p, h_p, tm=PROJ_ROWS)
            h_s = fin(o_s, g_s, h_s, tm=db)
            s_p_list.append(s_p)
            s_s_list.append(s_s)

    y_prompt = h_p.reshape(batch, lp, D_MODEL)[:, N_META:length]
    kv_shape = (batch, DEPTH // 2, length, A_HEADS, A_HD2)
    return (y_prompt, h_s.reshape(db, 1, D_MODEL),
            kv_p[0].reshape(kv_shape), kv_p[1].reshape(kv_shape), jnp.stack(s_p_list, axis=1),
            jnp.stack(k_s_rows, axis=1), jnp.stack(v_s_rows, axis=1), jnp.stack(s_s_list, axis=1))
```

```python
import functools
import math

import numpy as np
import jax
import jax.numpy as jnp
from jax import lax
from jax.experimental import pallas as pl
from jax.experimental.pallas import tpu as pltpu

F32 = jnp.float32
BF16 = jnp.bfloat16

D_MODEL = 1024
DEPTH = 4
N_META = 16
PAGE_SIZE = 128
A_HEADS = 8
A_HEAD_DIM = 64
A_HD2 = 2 * A_HEAD_DIM
A_QK = A_HEADS * A_HD2
G_HEADS = 4
G_DK = 128
G_DV = 256
G_QK = G_HEADS * G_DK
G_V = G_HEADS * G_DV
G_RANK = 16
G_RANK_PAD = 128
G_NORMALIZER = 16.0
G_CHUNK = 64
DN_ALPHA = (2.0 * DEPTH) ** 0.25
LN_EPS = 1e-5
RMS_EPS = 1e-5

MASK_VALUE = -1e30
LOG2E = math.log2(math.e)
ATT_TILE = 256
ATT_Q = 2 * ATT_TILE
VT_ROWS = A_HD2 + 16
KA_COLS = A_HD2 + 16
GLA_ROWS = 128
PROJ_ROWS = 512
DEC_PAGES = 8
VMEM_LIMIT = 56 * 1024 * 1024

NT_DIMS = (((1,), (1,)), ((), ()))
TN_DIMS = (((0,), (0,)), ((), ()))


def _cparams(sem):
    return pltpu.CompilerParams(dimension_semantics=sem, vmem_limit_bytes=VMEM_LIMIT)


def _proj_tile(step, tiles_per_seq, last_kv_tile):
    return jnp.where(step >= last_kv_tile, tiles_per_seq - 1 + last_kv_tile - step, step)


def _attn_proj_kernel(x_ref, w_ref, *refs, tiles_per_seq, last_kv_tile, prev_kv):
    n_in = 2 if prev_kv else 0
    q_ref, k_ref, v_ref, g_ref = refs[n_in:n_in + 4]
    flash_refs = refs[:n_in] + refs[n_in + 4:]
    x = x_ref[...].astype(BF16)

    def cols(c):
        return jnp.dot(x, w_ref[:, c * A_QK:(c + 1) * A_QK], preferred_element_type=F32)

    q_ref[...] = (cols(0) * (A_HEAD_DIM ** -0.5 * LOG2E)).astype(BF16)
    k = cols(1)
    v = cols(2)
    g_ref[...] = cols(3)
    if tiles_per_seq is None:
        k_ref[...] = k
        v_ref[...] = v
        return
    if prev_kv:
        kprev_ref, vprev_ref, ka_ref, vt_ref = flash_refs
        k_ref[0] = kprev_ref[...]
        v_ref[0] = vprev_ref[...]
        k_ref[1] = k
        v_ref[1] = v
    else:
        ka_ref, vt_ref = flash_refs
        k_ref[...] = k
        v_ref[...] = v
    t = ATT_TILE
    tile = _proj_tile(pl.program_id(0) % tiles_per_seq, tiles_per_seq, last_kv_tile)
    pos_lo = lax.broadcasted_iota(jnp.int32, (t, KA_COLS - A_HD2), 0).astype(F32)
    lane = lax.broadcasted_iota(jnp.int32, (t, KA_COLS - A_HD2), 1)
    parity = (tile % 2).astype(F32)
    pos_cols = jnp.where(lane < 3, pos_lo, jnp.where(lane < 6, parity, 0.0)).astype(BF16)
    sub = lax.broadcasted_iota(jnp.int32, (VT_ROWS - A_HD2, t), 0)
    ones_rows = jnp.where(sub == 0, 1.0, 0.0).astype(BF16)
    for h in range(A_HEADS):
        hs = slice(h * A_HD2, (h + 1) * A_HD2)
        ka_ref[h, :, 0:A_HD2] = k[:, hs].astype(BF16)
        ka_ref[h, :, A_HD2:KA_COLS] = pos_cols
        vt_ref[h, 0:A_HD2, :] = v[:, hs].T.astype(BF16)
        vt_ref[h, A_HD2:VT_ROWS, :] = ones_rows


def attn_proj(x, w_bf, tm, batch=None, valid_len=None, prev_kv=()):
    m = x.shape[0]
    full = lambda i: (0, 0)
    in_specs = [None, pl.BlockSpec((D_MODEL, 4 * A_QK), full)]
    if batch is None:
        tiles_per_seq = last_kv_tile = None
        row = lambda i: (i, 0)
        blk = pl.BlockSpec((tm, A_QK), row)
        sds = lambda dt: jax.ShapeDtypeStruct((m, A_QK), dt)
        out_specs = [blk] * 4
        out_shape = [sds(BF16), sds(F32), sds(F32), sds(F32)]
    else:
        assert tm == ATT_TILE
        tiles_per_seq = m // batch // tm
        last_kv_tile = (valid_len - 1) // tm
        seq = lambda i: i // tiles_per_seq
        tile = lambda i: _proj_tile(i % tiles_per_seq, tiles_per_seq, last_kv_tile)
        row = lambda i: (seq(i) * tiles_per_seq + tile(i), 0)
        blk = pl.BlockSpec((tm, A_QK), row)
        sds = lambda dt: jax.ShapeDtypeStruct((m, A_QK), dt)
        kv_tile = lambda i: jnp.minimum(tile(i), last_kv_tile)
        kv_in = pl.BlockSpec((None, tm, A_QK), lambda i: (seq(i), kv_tile(i), 0))
        if prev_kv:
            kv_out = pl.BlockSpec((None, 2, tm, A_QK), lambda i: (seq(i), 0, kv_tile(i), 0))
            kv_sds = jax.ShapeDtypeStruct((batch, 2, valid_len, A_QK), F32)
            in_specs += [kv_in, kv_in]
        else:
            kv_out = kv_in
            kv_sds = jax.ShapeDtypeStruct((batch, valid_len, A_QK), F32)
        out_specs = [blk, kv_out, kv_out, blk,
                     pl.BlockSpec((A_HEADS, tm, KA_COLS), lambda i: (0, row(i)[0], 0)),
                     pl.BlockSpec((None, A_HEADS, None, VT_ROWS, tm), lambda i: (seq(i), 0, tile(i), 0, 0))]
        out_shape = [sds(BF16), kv_sds, kv_sds, sds(F32),
                     jax.ShapeDtypeStruct((A_HEADS, m, KA_COLS), BF16),
                     jax.ShapeDtypeStruct((batch, A_HEADS, tiles_per_seq, VT_ROWS, tm), BF16)]
    in_specs[0] = pl.BlockSpec((tm, D_MODEL), row)
    return pl.pallas_call(
        functools.partial(_attn_proj_kernel, tiles_per_seq=tiles_per_seq, last_kv_tile=last_kv_tile,
                          prev_kv=bool(prev_kv)),
        grid=(m // tm,),
        in_specs=in_specs,
        out_specs=out_specs,
        out_shape=out_shape,
        compiler_params=_cparams(("arbitrary",)),
        name="attn_proj",
    )(x, w_bf, *prev_kv)


def _log_sigmoid(x):
    return jnp.minimum(x, 0.0) - jnp.log1p(jnp.exp(-jnp.abs(x)))


def _gla_proj_kernel(x_ref, w_ref, wr_ref, wup_ref, bg_ref, q_ref, k_ref, v_ref, g_ref, la_ref):
    x = x_ref[...].astype(BF16)

    def cols(a, b):
        return jnp.dot(x, w_ref[:, a:b], preferred_element_type=F32)

    q_ref[...] = cols(0, G_QK) * (G_DK ** -0.5)
    k_ref[...] = cols(G_QK, 2 * G_QK)
    v_ref[...] = cols(2 * G_QK, 2 * G_QK + G_V)
    g_ref[...] = cols(2 * G_QK + G_V, 2 * G_QK + 2 * G_V)
    r = jnp.dot(x, wr_ref[...], preferred_element_type=F32)
    gate = jnp.dot(r.astype(BF16), wup_ref[...], preferred_element_type=F32) + bg_ref[...]
    la_ref[...] = _log_sigmoid(gate) * (LOG2E / G_NORMALIZER)


def gla_proj(x, w_bf, wr_bf, wup_bf, b_gate, tm):
    m = x.shape[0]
    row = lambda i: (i, 0)
    full = lambda i: (0, 0)
    return pl.pallas_call(
        _gla_proj_kernel,
        grid=(m // tm,),
        in_specs=[pl.BlockSpec((tm, D_MODEL), row),
                  pl.BlockSpec((D_MODEL, 2 * G_QK + 2 * G_V), full),
                  pl.BlockSpec((D_MODEL, G_RANK_PAD), full),
                  pl.BlockSpec((G_RANK_PAD, G_QK), full),
                  pl.BlockSpec((1, G_QK), full)],
        out_specs=[pl.BlockSpec((tm, G_QK), row), pl.BlockSpec((tm, G_QK), row),
                   pl.BlockSpec((tm, G_V), row), pl.BlockSpec((tm, G_V), row),
                   pl.BlockSpec((tm, G_QK), row)],
        out_shape=[jax.ShapeDtypeStruct((m, G_QK), F32), jax.ShapeDtypeStruct((m, G_QK), F32),
                   jax.ShapeDtypeStruct((m, G_V), F32), jax.ShapeDtypeStruct((m, G_V), F32),
                   jax.ShapeDtypeStruct((m, G_QK), F32)],
        compiler_params=_cparams(("parallel",)),
        name="gla_proj",
    )(x, w_bf, wr_bf, wup_bf, b_gate)


def _finish_kernel(o_ref, g_ref, h_ref, w_ref, nw_ref, lg_ref, lb_ref, out_ref, *, head_dim, scale):
    o = o_ref[...]
    nw = nw_ref[...]
    parts = []
    for c in range(o.shape[1] // head_dim):
        oh = o[:, c * head_dim:(c + 1) * head_dim]
        ms = jnp.mean(oh * oh, axis=-1, keepdims=True)
        parts.append(oh * lax.rsqrt(ms + RMS_EPS) * nw)
    on = jnp.concatenate(parts, axis=1)
    if scale != 1.0:
        on = on * scale
    g = g_ref[...]
    y = on * (g * jax.nn.sigmoid(g))
    y = jnp.dot(y.astype(BF16), w_ref[...], preferred_element_type=F32)
    x = DN_ALPHA * h_ref[...] + y
    mu = jnp.mean(x, axis=-1, keepdims=True)
    xc = x - mu
    var = jnp.mean(xc * xc, axis=-1, keepdims=True)
    out_ref[...] = xc * lax.rsqrt(var + LN_EPS) * lg_ref[...] + lb_ref[...]


def finish(o, g, h, w_bf, norm_w, ln_g, ln_b, head_dim, scale, tm):
    m, n = o.shape
    row = lambda i: (i, 0)
    full = lambda i: (0, 0)
    return pl.pallas_call(
        functools.partial(_finish_kernel, head_dim=head_dim, scale=scale),
        grid=(m // tm,),
        in_specs=[pl.BlockSpec((tm, n), row), pl.BlockSpec((tm, n), row),
                  pl.BlockSpec((tm, D_MODEL), row),
                  pl.BlockSpec((n, D_MODEL), full),
                  pl.BlockSpec((1, head_dim), full),
                  pl.BlockSpec((1, D_MODEL), full), pl.BlockSpec((1, D_MODEL), full)],
        out_specs=pl.BlockSpec((tm, D_MODEL), row),
        out_shape=jax.ShapeDtypeStruct((m, D_MODEL), F32),
        compiler_params=_cparams(("parallel",)),
        name="finish",
    )(o, g, h, w_bf, norm_w.reshape(1, head_dim), ln_g.reshape(1, D_MODEL), ln_b.reshape(1, D_MODEL))


def _lambda_value(lam_ref, lam_init):
    lv = lam_ref[...]
    e1 = jnp.exp(jnp.sum(lv[0:1] * lv[1:2], axis=-1, keepdims=True))
    e2 = jnp.exp(jnp.sum(lv[2:3] * lv[3:4], axis=-1, keepdims=True))
    return e1 - e2 + lam_init


def _split_maps(q):
    lane = lax.broadcasted_iota(jnp.int32, q.shape, 1)
    zero = jnp.zeros_like(q)
    return jnp.concatenate([jnp.where(lane < A_HEAD_DIM, q, zero),
                            jnp.where(lane >= A_HEAD_DIM, q, zero)], axis=0)


def _bf16_terms(x):
    t1 = x.astype(BF16).astype(F32)
    r1 = x - t1
    t2 = r1.astype(BF16).astype(F32)
    t3 = (r1 - t2).astype(BF16).astype(F32)
    return t1, t2, t3


def _flash_kernel(c_ref, lam_ref, q_ref, ka_ref, vt_ref, o_ref, m_sc, acc_sc, sa_sc, ma_sc, sb_sc, mb_sc,
                  *, lam_init, last_rows):
    t = ATT_TILE
    w = ATT_Q
    h = pl.program_id(1)
    qi = pl.program_id(2)
    c = c_ref[h]
    sub = lax.broadcasted_iota(jnp.int32, (KA_COLS - A_HD2, 1), 0)
    coef = jnp.where(sub < 3, c, jnp.where(sub < 6, c * t, 0.0))
    t1, t2, t3 = _bf16_terms(coef)
    part0 = (sub == 0) | (sub == 3)
    part1 = (sub == 1) | (sub == 4)
    coef = jnp.where(part0, t1, jnp.where(part1, t2, t3))

    def attend(nq_rows):
        cols = 2 * nq_rows
        q_t = _split_maps(q_ref[0:nq_rows, :].astype(F32)).T
        qa_t = jnp.concatenate([q_t, jnp.broadcast_to(coef, (KA_COLS - A_HD2, cols))], axis=0).astype(BF16)

        def scores(u):
            start = pl.multiple_of(u * w, w)
            return jnp.dot(ka_ref[pl.ds(start, w), :], qa_t, preferred_element_type=F32)

        def prefetch(u, s_ref, smax_ref):
            s = scores(u)
            s_ref[:, 0:cols] = s
            smax_ref[:, 0:cols] = jnp.max(s, axis=0, keepdims=True)

        def accumulate(s, s_max, u):
            offset = c * (w * (u - qi)).astype(F32)
            m_old = m_sc[:, 0:cols]
            m_rel = jnp.maximum(m_old - offset, s_max)
            m_new = m_rel + offset
            alpha = jnp.exp2(m_old - m_new)
            p = jnp.exp2(s - m_rel).astype(BF16)
            pv = jnp.dot(vt_ref[2 * u], p[0:t], preferred_element_type=F32)
            pv += jnp.dot(vt_ref[2 * u + 1], p[t:w], preferred_element_type=F32)
            acc_sc[:, 0:cols] = alpha * acc_sc[:, 0:cols] + pv
            m_sc[:, 0:cols] = m_new

        def accumulate_diagonal(s):
            key = lax.broadcasted_iota(jnp.int32, (w, cols), 0)
            col = lax.broadcasted_iota(jnp.int32, (w, cols), 1)
            s = jnp.where(key <= jnp.where(col >= nq_rows, col - nq_rows, col), s, MASK_VALUE)
            accumulate(s, jnp.max(s, axis=0, keepdims=True), qi)

        m_sc[...] = jnp.full_like(m_sc, MASK_VALUE)
        acc_sc[...] = jnp.zeros_like(acc_sc)
        prefetch(0, sa_sc, ma_sc)

        def body(jj, carry):
            u = 2 * jj
            prefetch(u + 1, sb_sc, mb_sc)
            accumulate(sa_sc[:, 0:cols], ma_sc[:, 0:cols], u)
            prefetch(u + 2, sa_sc, ma_sc)
            accumulate(sb_sc[:, 0:cols], mb_sc[:, 0:cols], u + 1)
            return carry

        lax.fori_loop(0, qi // 2, body, 0)

        @pl.when(qi % 2 == 1)
        def _():
            s_diag = scores(qi)
            accumulate(sa_sc[:, 0:cols], ma_sc[:, 0:cols], qi - 1)
            accumulate_diagonal(s_diag)

        @pl.when(qi % 2 == 0)
        def _():
            accumulate_diagonal(sa_sc[:, 0:cols])

        lam = _lambda_value(lam_ref, lam_init)
        acc = acc_sc[:, 0:cols]
        o = acc[0:A_HD2] / acc[A_HD2:A_HD2 + 1]
        o_ref[0:nq_rows, :] = (o[:, :nq_rows] - lam * o[:, nq_rows:]).T
        if nq_rows < w:
            o_ref[nq_rows:w, :] = jnp.zeros((w - nq_rows, A_HD2), F32)

    if last_rows == w:
        attend(w)
    else:
        is_last = qi == pl.num_programs(2) - 1
        pl.when(jnp.logical_not(is_last))(lambda: attend(w))
        pl.when(is_last)(lambda: attend(last_rows))


def flash_prompt(q, ka, vt, lam_rows, lam_init, batch, lp, valid_len):
    w = ATT_Q
    nq = lp // w
    lane_tile = A_HD2
    last_rows = min(w, -(-(valid_len - (nq - 1) * w) // lane_tile) * lane_tile)
    coefs = np.array([2.0 ** (-8.0 * (i + 1) / A_HEADS) * LOG2E for i in range(A_HEADS)], np.float32)
    return pl.pallas_call(
        functools.partial(_flash_kernel, lam_init=lam_init, last_rows=last_rows),
        grid=(batch, A_HEADS, nq),
        in_specs=[pl.BlockSpec(memory_space=pltpu.SMEM),
                  pl.BlockSpec((4, A_HEAD_DIM), lambda b, h, i: (0, 0)),
                  pl.BlockSpec((w, A_HD2), lambda b, h, i: (b * nq + i, h)),
                  pl.BlockSpec((None, lp, KA_COLS), lambda b, h, i: (h, b, 0)),
                  pl.BlockSpec((None, None, lp // ATT_TILE, VT_ROWS, ATT_TILE),
                               lambda b, h, i: (b, h, 0, 0, 0))],
        out_specs=pl.BlockSpec((w, A_HD2), lambda b, h, i: (b * nq + i, h)),
        out_shape=jax.ShapeDtypeStruct((batch * lp, A_QK), F32),
        scratch_shapes=[pltpu.VMEM((1, 2 * w), F32), pltpu.VMEM((VT_ROWS, 2 * w), F32),
                        pltpu.VMEM((w, 2 * w), F32), pltpu.VMEM((1, 2 * w), F32),
                        pltpu.VMEM((w, 2 * w), F32), pltpu.VMEM((1, 2 * w), F32)],
        compiler_params=_cparams(("parallel", "parallel", "arbitrary")),
        name="flash_prompt",
    )(jnp.asarray(coefs), lam_rows, q, ka, vt)


def _decode_kernel(pt_ref, lam_ref, q_ref, kn_ref, vn_ref, bias_ref, slope_ref, *rest, lam_init, past_len):
    del pt_ref
    k_refs = rest[:DEC_PAGES]
    v_refs = rest[DEC_PAGES:2 * DEC_PAGES]
    o_ref, m_sc, l_sc, acc_sc = rest[2 * DEC_PAGES:]
    j = pl.program_id(1)
    qq = _split_maps(q_ref[...]).astype(BF16)
    slope_col = slope_ref[...]

    @pl.when(j == 0)
    def _():
        kn = kn_ref[...]
        kn2 = jnp.concatenate([kn, kn], axis=0)
        m_sc[...] = jnp.sum(qq.astype(F32) * kn2, axis=-1, keepdims=True)
        l_sc[...] = jnp.ones_like(l_sc)
        vn = vn_ref[...]
        acc_sc[...] = jnp.concatenate([vn, vn], axis=0)

    bias = bias_ref[...]
    width = PAGE_SIZE * A_HEADS
    s_pages = []
    for i in range(DEC_PAGES):
        page = j * DEC_PAGES + i
        k = k_refs[i][...].reshape(width, A_HD2).astype(BF16)
        offset = slope_col * (page * PAGE_SIZE - past_len).astype(F32)
        s_pages.append(lax.dot_general(qq, k, NT_DIMS, preferred_element_type=F32) + (bias + offset))
    m_old = m_sc[...]
    m_new = m_old
    for s in s_pages:
        m_new = jnp.maximum(m_new, jnp.max(s, axis=-1, keepdims=True))
    alpha = jnp.exp2(m_old - m_new)
    l_new = alpha * l_sc[...]
    acc = alpha * acc_sc[...]
    for i, s in enumerate(s_pages):
        p = jnp.exp2(s - m_new)
        l_new += jnp.sum(p, axis=-1, keepdims=True)
        v = v_refs[i][...].reshape(width, A_HD2).astype(BF16)
        acc += jnp.dot(p.astype(BF16), v, preferred_element_type=F32)
    l_sc[...] = l_new
    acc_sc[...] = acc
    m_sc[...] = m_new

    @pl.when(j == pl.num_programs(1) - 1)
    def _():
        lam = _lambda_value(lam_ref, lam_init)
        o = acc_sc[...] / l_sc[...]
        o_ref[...] = o[:A_HEADS] - lam * o[A_HEADS:]


def decode_attn(q_s, k_s, v_s, cache_k, cache_v, page_table, layer, lam_rows, lam_init):
    db = q_s.shape[0]
    n_pages = page_table.shape[1]
    past_len = n_pages * PAGE_SIZE
    slopes = np.array([2.0 ** (-8.0 * (i + 1) / A_HEADS) * LOG2E for i in range(A_HEADS)], np.float32)
    rows_h = np.tile(np.arange(A_HEADS), 2)
    cols_t = np.repeat(np.arange(PAGE_SIZE), A_HEADS)
    cols_h = np.tile(np.arange(A_HEADS), PAGE_SIZE)
    bias = np.where(rows_h[:, None] == cols_h[None, :],
                    slopes[rows_h][:, None] * cols_t[None, :].astype(np.float32),
                    np.float32(MASK_VALUE)).astype(np.float32)
    slope_col = slopes[rows_h][:, None]

    hd = lambda a: a.reshape(db, A_HEADS, A_HD2)
    per_b = pl.BlockSpec((None, A_HEADS, A_HD2), lambda b, j, pt: (b, 0, 0))
    const2 = lambda b, j, pt: (0, 0)

    def page_spec(i):
        return pl.BlockSpec((None, None, PAGE_SIZE, A_HEADS, A_HD2),
                            lambda b, j, pt: (pt[b, j * DEC_PAGES + i], layer, 0, 0, 0))

    grid_spec = pltpu.PrefetchScalarGridSpec(
        num_scalar_prefetch=1,
        grid=(db, n_pages // DEC_PAGES),
        in_specs=[pl.BlockSpec((4, A_HEAD_DIM), const2), per_b, per_b, per_b,
                  pl.BlockSpec((2 * A_HEADS, PAGE_SIZE * A_HEADS), const2),
                  pl.BlockSpec((2 * A_HEADS, 1), const2)]
                 + [page_spec(i) for i in range(DEC_PAGES)] * 2,
        out_specs=per_b,
        scratch_shapes=[pltpu.VMEM((2 * A_HEADS, 1), F32), pltpu.VMEM((2 * A_HEADS, 1), F32),
                        pltpu.VMEM((2 * A_HEADS, A_HD2), F32)],
    )
    out = pl.pallas_call(
        functools.partial(_decode_kernel, lam_init=lam_init, past_len=past_len),
        grid_spec=grid_spec,
        out_shape=jax.ShapeDtypeStruct((db, A_HEADS, A_HD2), F32),
        compiler_params=_cparams(("parallel", "arbitrary")),
        name="decode_attn",
    )(page_table, lam_rows, hd(q_s.astype(F32)), hd(k_s), hd(v_s), jnp.asarray(bias), jnp.asarray(slope_col),
      *([cache_k] * DEC_PAGES), *([cache_v] * DEC_PAGES))
    return out.reshape(db, A_QK)


G_LEVELS = tuple(G_CHUNK >> (i + 1) for i in range(G_CHUNK.bit_length() - 1))


def _gla_decay_matrix():
    c = G_CHUNK
    r = np.arange(c)[:, None]
    j = np.arange(c)[None, :]
    blocks = [j <= r, j > r]
    for blk in G_LEVELS:
        ref = (r // (2 * blk)) * (2 * blk) + blk
        blocks.append(np.where(r >= ref, (j > ref) & (j <= r), (j > r) & (j <= ref)))
    return np.concatenate(blocks, axis=0).astype(np.float32)


def _gla_level_masks():
    c = G_CHUNK
    n = G_HEADS * c
    row = lax.broadcasted_iota(jnp.int32, (c, 1), 0)
    t_i = lax.broadcasted_iota(jnp.int32, (n, n), 0)
    s_i = lax.broadcasted_iota(jnp.int32, (n, n), 1)
    later, owns = [], []
    for blk in G_LEVELS:
        shift = blk.bit_length()
        later.append((row & blk) != 0)
        owns.append(((t_i >> shift) == (s_i >> shift)) & ((t_i & blk) != 0) & ((s_i & blk) == 0))
    return later, owns, t_i == s_i


def _heads_to_rows(x, width):
    return jnp.concatenate([x[:, h * width:(h + 1) * width] for h in range(G_HEADS)], axis=0)


def _gla_chunk(q, k, v, la, states, dmat, masks):
    c = q.shape[0]
    later, owns, diagonal = masks
    l1 = la.astype(BF16)
    l2 = (la - l1.astype(F32)).astype(BF16)
    e = jnp.dot(dmat, l1, preferred_element_type=F32) + jnp.dot(dmat, l2, preferred_element_type=F32)
    b = e[0:c]
    vb = v.astype(BF16)

    qe = (q * jnp.exp2(b)).astype(BF16)
    qk = _heads_to_rows(q * k, G_DK)
    a = jnp.where(diagonal, jnp.sum(qk, axis=-1, keepdims=True), 0.0)
    for i in range(len(G_LEVELS)):
        x = (jnp.where(later[i], q, k) * jnp.exp2(e[(2 + i) * c:(3 + i) * c])).astype(BF16)
        xs = _heads_to_rows(x, G_DK)
        a = jnp.where(owns[i], lax.dot_general(xs, xs, NT_DIMS, preferred_element_type=F32), a)
    o_intra = jnp.dot(a.astype(BF16), _heads_to_rows(vb, G_DV), preferred_element_type=F32)

    k_dec = (k * jnp.exp2(e[c:2 * c])).astype(BF16)
    decay = jnp.exp2(b[c - 1:c])
    outs, new_states = [], []
    for h in range(G_HEADS):
        ks = slice(h * G_DK, (h + 1) * G_DK)
        st = states[h]
        o_inter = lax.dot_general(qe[:, ks], st.astype(BF16), NT_DIMS, preferred_element_type=F32)
        outs.append(o_inter + o_intra[h * c:(h + 1) * c])
        new_states.append(st * decay[:, ks] + lax.dot_general(
            vb[:, h * G_DV:(h + 1) * G_DV], k_dec[:, ks], TN_DIMS, preferred_element_type=F32))
    return outs, new_states


def _gla_kernel(dmat_ref, q_ref, k_ref, v_ref, la_ref, o_ref, s_ref, st_sc, *, valid_len):
    i = pl.program_id(1)

    @pl.when(i == 0)
    def _():
        st_sc[...] = jnp.zeros_like(st_sc)

    has_real_rows = i * GLA_ROWS < valid_len

    @pl.when(has_real_rows)
    def _():
        dmat = dmat_ref[...]
        masks = _gla_level_masks()
        states = [st_sc[h] for h in range(G_HEADS)]
        for c in range(GLA_ROWS // G_CHUNK):
            sl = slice(c * G_CHUNK, (c + 1) * G_CHUNK)
            pos = i * GLA_ROWS + c * G_CHUNK + lax.broadcasted_iota(jnp.int32, (G_CHUNK, 1), 0)
            valid = pos < valid_len
            la = jnp.where(valid, la_ref[sl, :], 0.0)
            k = jnp.where(valid, k_ref[sl, :], 0.0)
            outs, states = _gla_chunk(q_ref[sl, :], k, v_ref[sl, :], la, states, dmat, masks)
            for h in range(G_HEADS):
                o_ref[sl, h * G_DV:(h + 1) * G_DV] = outs[h]
        for h in range(G_HEADS):
            st_sc[h] = states[h]

    @pl.when(jnp.logical_not(has_real_rows))
    def _():
        o_ref[...] = jnp.zeros_like(o_ref)

    @pl.when(i == pl.num_programs(1) - 1)
    def _():
        for h in range(G_HEADS):
            s_ref[h] = st_sc[h].T


def gla_prompt(q, k, v, la, batch, lp, valid_len):
    nblk = lp // GLA_ROWS
    qk_spec = pl.BlockSpec((GLA_ROWS, G_QK), lambda b, i: (b * nblk + i, 0))
    v_spec = pl.BlockSpec((GLA_ROWS, G_V), lambda b, i: (b * nblk + i, 0))
    dmat = jnp.asarray(_gla_decay_matrix(), BF16)
    return pl.pallas_call(
        functools.partial(_gla_kernel, valid_len=valid_len),
        grid=(batch, nblk),
        in_specs=[pl.BlockSpec(dmat.shape, lambda b, i: (0, 0)), qk_spec, qk_spec, v_spec, qk_spec],
        out_specs=[v_spec, pl.BlockSpec((None, G_HEADS, G_DK, G_DV), lambda b, i: (b, 0, 0, 0))],
        out_shape=[jax.ShapeDtypeStruct((batch * lp, G_V), F32),
                   jax.ShapeDtypeStruct((batch, G_HEADS, G_DK, G_DV), F32)],
        scratch_shapes=[pltpu.VMEM((G_HEADS, G_DV, G_DK), F32)],
        compiler_params=_cparams(("parallel", "arbitrary")),
        name="gla_prompt",
    )(dmat, q, k, v, la)


def _stack_rows(rows, n_rows=16):
    n = rows[0].shape[1]
    idx = lax.broadcasted_iota(jnp.int32, (n_rows, n), 0)
    out = jnp.zeros((n_rows, n), F32)
    for r, x in enumerate(rows):
        out = jnp.where(idx == r, x, out)
    return out.astype(BF16)


def _gla_step_kernel(q_ref, k_ref, v_ref, la_ref, s_ref, o_ref, so_ref):
    q = q_ref[...]
    k = k_ref[...]
    v = v_ref[...]
    a = jnp.exp2(la_ref[...])
    ones = jnp.ones((16, G_DV), BF16)
    o_parts = []
    for h in range(G_HEADS):
        ks = slice(h * G_DK, (h + 1) * G_DK)
        vs = slice(h * G_DV, (h + 1) * G_DV)
        a_h = a[:, ks]
        a1 = a_h.astype(BF16)
        r1 = a_h - a1.astype(F32)
        a2 = r1.astype(BF16)
        a3 = (r1 - a2.astype(F32)).astype(BF16)
        a_rows = _stack_rows([a1.astype(F32), a2.astype(F32), a3.astype(F32)])
        a_col = lax.dot_general(a_rows, ones, TN_DIMS, preferred_element_type=F32)
        kv = lax.dot_general(_stack_rows([k[:, ks]]), _stack_rows([v[:, vs]]), TN_DIMS,
                             preferred_element_type=F32)
        s1 = a_col * s_ref[h] + kv
        so_ref[h] = s1
        o = jnp.dot(_stack_rows([q[:, ks]]), s1.astype(BF16), preferred_element_type=F32)
        o_parts.append(o[0:1])
    o_ref[...] = jnp.concatenate(o_parts, axis=1)


def gla_step(q_s, k_s, v_s, la_s, state_gla, layer):
    db = q_s.shape[0]
    r3 = lambda a: a.reshape(db, 1, a.shape[1])
    vec = lambda n: pl.BlockSpec((None, 1, n), lambda b: (b, 0, 0))
    o, s_new = pl.pallas_call(
        _gla_step_kernel,
        grid=(db,),
        in_specs=[vec(G_QK), vec(G_QK), vec(G_V), vec(G_QK),
                  pl.BlockSpec((None, None, G_HEADS, G_DK, G_DV), lambda b: (b, layer, 0, 0, 0))],
        out_specs=[vec(G_V), pl.BlockSpec((None, G_HEADS, G_DK, G_DV), lambda b: (b, 0, 0, 0))],
        out_shape=[jax.ShapeDtypeStruct((db, 1, G_V), F32),
                   jax.ShapeDtypeStruct((db, G_HEADS, G_DK, G_DV), F32)],
        compiler_params=_cparams(("parallel",)),
        name="gla_step",
    )(r3(q_s), r3(k_s), r3(v_s), r3(la_s), state_gla)
    return o.reshape(db, G_V), s_new


def kernel(x_prompt, x_sample, cache_k, cache_v, state_gla, page_table, meta_tokens, attn_w_in, attn_lq1, attn_lk1, attn_lq2, attn_lk2, attn_subln_w, attn_w_out, gla_w_in, gla_w_gate_up, gla_b_gate, gla_norm_w, gla_w_out, ln_g, ln_b):
    batch, seq, _ = x_prompt.shape
    db = x_sample.shape[0]
    length = N_META + seq
    lp = -(-length // ATT_Q) * ATT_Q
    meta = jnp.broadcast_to(meta_tokens[None].astype(x_prompt.dtype), (batch, N_META, D_MODEL))
    h_p = lax.pad(x_prompt, jnp.zeros((), x_prompt.dtype), ((0, 0, 0), (N_META, lp - length, 0), (0, 0, 0)))
    h_p = lax.dynamic_update_slice(h_p, meta, (0, 0, 0)).reshape(batch * lp, D_MODEL)
    h_s = x_sample.reshape(db, D_MODEL)

    assert DEPTH == 4
    kv_p = ()
    k_s_rows, v_s_rows, s_p_list, s_s_list = [], [], [], []
    for i in range(DEPTH):
        if i % 2 == 0:
            a = i // 2
            lam_init = 0.8 - 0.6 * math.exp(-0.3 * i)
            w_in = attn_w_in[a].astype(BF16)
            w_out = attn_w_out[a].astype(BF16)
            lam_rows = jnp.stack([attn_lq1[a], attn_lk1[a], attn_lq2[a], attn_lk2[a]]).astype(F32)
            q_p, k_p, v_p, g_p, ka_p, vt_p = attn_proj(h_p, w_in, ATT_TILE, batch, length, kv_p)
            kv_p = (k_p, v_p)
            q_s, k_s, v_s, g_s = attn_proj(h_s, w_in, db)
            o_p = flash_prompt(q_p, ka_p, vt_p, lam_rows, lam_init, batch, lp, length)
            o_s = decode_attn(q_s, k_s, v_s, cache_k, cache_v, page_table, a, lam_rows, lam_init)
            fin = functools.partial(finish, w_bf=w_out, norm_w=attn_subln_w[a], ln_g=ln_g[i], ln_b=ln_b[i],
                                    head_dim=A_HD2, scale=1.0 - lam_init)
            h_p = fin(o_p, g_p, h_p, tm=PROJ_ROWS)
            h_s = fin(o_s, g_s, h_s, tm=db)
            k_s_rows.append(k_s.reshape(db, 1, A_HEADS, A_HD2))
            v_s_rows.append(v_s.reshape(db, 1, A_HEADS, A_HD2))
        else:
            gi = i // 2
            extra = G_RANK_PAD - G_RANK
            n_main = 2 * G_QK + 2 * G_V
            w_in = gla_w_in[gi][:, :n_main].astype(BF16)
            w_rank = jnp.pad(gla_w_in[gi][:, n_main:], ((0, 0), (0, extra))).astype(BF16)
            w_up = jnp.pad(gla_w_gate_up[gi], ((0, extra), (0, 0))).astype(BF16)
            w_out = gla_w_out[gi].astype(BF16)
            b_gate = gla_b_gate[gi].reshape(1, G_QK).astype(F32)
            q_p, k_p, v_p, g_p, la_p = gla_proj(h_p, w_in, w_rank, w_up, b_gate, PROJ_ROWS)
            q_s, k_s, v_s, g_s, la_s = gla_proj(h_s, w_in, w_rank, w_up, b_gate, db)
            o_p, s_p = gla_prompt(q_p, k_p, v_p, la_p, batch, lp, length)
            o_s, s_s = gla_step(q_s, k_s, v_s, la_s, state_gla, gi)
            fin = functools.partial(finish, w_bf=w_out, norm_w=gla_norm_w[gi], ln_g=ln_g[i], ln_b=ln_b[i],
                                    head_dim=G_DV, scale=1.0)
            h_p = fin(o_p, g_p, h_p, tm=PROJ_ROWS)
            h_s = fin(o_s, g_s, h_s, tm=db)
            s_p_list.append(s_p)
            s_s_list.append(s_s)

    y_prompt = h_p.reshape(batch, lp, D_MODEL)[:, N_META:length]
    kv_shape = (batch, DEPTH // 2, length, A_HEADS, A_HD2)
    return (y_prompt, h_s.reshape(db, 1, D_MODEL),
            kv_p[0].reshape(kv_shape), kv_p[1].reshape(kv_shape), jnp.stack(s_p_list, axis=1),
            jnp.stack(k_s_rows, axis=1), jnp.stack(v_s_rows, axis=1), jnp.stack(s_s_list, axis=1))
```

```python
import functools
import math

import numpy as np
import jax
import jax.numpy as jnp
from jax import lax
from jax.experimental import pallas as pl
from jax.experimental.pallas import tpu as pltpu

F32 = jnp.float32
BF16 = jnp.bfloat16

D_MODEL = 1024
DEPTH = 4
N_META = 16
PAGE_SIZE = 128
A_HEADS = 8
A_HEAD_DIM = 64
A_HD2 = 2 * A_HEAD_DIM
A_QK = A_HEADS * A_HD2
G_HEADS = 4
G_DK = 128
G_DV = 256
G_QK = G_HEADS * G_DK
G_V = G_HEADS * G_DV
G_RANK = 16
G_RANK_PAD = 128
G_NORMALIZER = 16.0
G_CHUNK = 64
DN_ALPHA = (2.0 * DEPTH) ** 0.25
LN_EPS = 1e-5
RMS_EPS = 1e-5

MASK_VALUE = -1e30
LOG2E = math.log2(math.e)
ATT_TILE = 256
ATT_Q = 2 * ATT_TILE
VT_ROWS = A_HD2 + 16
KA_COLS = A_HD2 + 16
GLA_ROWS = 256
PROJ_ROWS = 512
DEC_PAGES = 8
VMEM_LIMIT = 56 * 1024 * 1024

NT_DIMS = (((1,), (1,)), ((), ()))
TN_DIMS = (((0,), (0,)), ((), ()))


def _cparams(sem):
    return pltpu.CompilerParams(dimension_semantics=sem, vmem_limit_bytes=VMEM_LIMIT)


def _proj_tile(step, tiles_per_seq, last_kv_tile):
    return jnp.where(step >= last_kv_tile, tiles_per_seq - 1 + last_kv_tile - step, step)


def _attn_proj_kernel(x_ref, w_ref, *refs, tiles_per_seq, last_kv_tile, prev_kv):
    n_in = 2 if prev_kv else 0
    q_ref, k_ref, v_ref, g_ref = refs[n_in:n_in + 4]
    flash_refs = refs[:n_in] + refs[n_in + 4:]
    x = x_ref[...].astype(BF16)

    def cols(c):
        return jnp.dot(x, w_ref[:, c * A_QK:(c + 1) * A_QK], preferred_element_type=F32)

    q_ref[...] = (cols(0) * (A_HEAD_DIM ** -0.5 * LOG2E)).astype(BF16)
    k = cols(1)
    v = cols(2)
    g_ref[...] = cols(3)
    if tiles_per_seq is None:
        k_ref[...] = k
        v_ref[...] = v
        return
    if prev_kv:
        kprev_ref, vprev_ref, ka_ref, vt_ref = flash_refs
        k_ref[0] = kprev_ref[...]
        v_ref[0] = vprev_ref[...]
        k_ref[1] = k
        v_ref[1] = v
    else:
        ka_ref, vt_ref = flash_refs
        k_ref[...] = k
        v_ref[...] = v
    t = ATT_TILE
    tile = _proj_tile(pl.program_id(0) % tiles_per_seq, tiles_per_seq, last_kv_tile)
    pos_lo = lax.broadcasted_iota(jnp.int32, (t, KA_COLS - A_HD2), 0).astype(F32)
    lane = lax.broadcasted_iota(jnp.int32, (t, KA_COLS - A_HD2), 1)
    parity = (tile % 2).astype(F32)
    pos_cols = jnp.where(lane < 3, pos_lo, jnp.where(lane < 6, parity, 0.0)).astype(BF16)
    sub = lax.broadcasted_iota(jnp.int32, (VT_ROWS - A_HD2, t), 0)
    ones_rows = jnp.where(sub == 0, 1.0, 0.0).astype(BF16)
    for h in range(A_HEADS):
        hs = slice(h * A_HD2, (h + 1) * A_HD2)
        ka_ref[h, :, 0:A_HD2] = k[:, hs].astype(BF16)
        ka_ref[h, :, A_HD2:KA_COLS] = pos_cols
        vt_ref[h, 0:A_HD2, :] = v[:, hs].T.astype(BF16)
        vt_ref[h, A_HD2:VT_ROWS, :] = ones_rows


def attn_proj(x, w_bf, tm, batch=None, valid_len=None, prev_kv=()):
    m = x.shape[0]
    full = lambda i: (0, 0)
    in_specs = [None, pl.BlockSpec((D_MODEL, 4 * A_QK), full)]
    if batch is None:
        tiles_per_seq = last_kv_tile = None
        row = lambda i: (i, 0)
        blk = pl.BlockSpec((tm, A_QK), row)
        sds = lambda dt: jax.ShapeDtypeStruct((m, A_QK), dt)
        out_specs = [blk] * 4
        out_shape = [sds(BF16), sds(F32), sds(F32), sds(F32)]
    else:
        assert tm == ATT_TILE
        tiles_per_seq = m // batch // tm
        last_kv_tile = (valid_len - 1) // tm
        seq = lambda i: i // tiles_per_seq
        tile = lambda i: _proj_tile(i % tiles_per_seq, tiles_per_seq, last_kv_tile)
        row = lambda i: (seq(i) * tiles_per_seq + tile(i), 0)
        blk = pl.BlockSpec((tm, A_QK), row)
        sds = lambda dt: jax.ShapeDtypeStruct((m, A_QK), dt)
        kv_tile = lambda i: jnp.minimum(tile(i), last_kv_tile)
        kv_in = pl.BlockSpec((None, tm, A_QK), lambda i: (seq(i), kv_tile(i), 0))
        if prev_kv:
            kv_out = pl.BlockSpec((None, 2, tm, A_QK), lambda i: (seq(i), 0, kv_tile(i), 0))
            kv_sds = jax.ShapeDtypeStruct((batch, 2, valid_len, A_QK), F32)
            in_specs += [kv_in, kv_in]
        else:
            kv_out = kv_in
            kv_sds = jax.ShapeDtypeStruct((batch, valid_len, A_QK), F32)
        out_specs = [blk, kv_out, kv_out, blk,
                     pl.BlockSpec((A_HEADS, tm, KA_COLS), lambda i: (0, row(i)[0], 0)),
                     pl.BlockSpec((None, A_HEADS, None, VT_ROWS, tm), lambda i: (seq(i), 0, tile(i), 0, 0))]
        out_shape = [sds(BF16), kv_sds, kv_sds, sds(F32),
                     jax.ShapeDtypeStruct((A_HEADS, m, KA_COLS), BF16),
                     jax.ShapeDtypeStruct((batch, A_HEADS, tiles_per_seq, VT_ROWS, tm), BF16)]
    in_specs[0] = pl.BlockSpec((tm, D_MODEL), row)
    return pl.pallas_call(
        functools.partial(_attn_proj_kernel, tiles_per_seq=tiles_per_seq, last_kv_tile=last_kv_tile,
                          prev_kv=bool(prev_kv)),
        grid=(m // tm,),
        in_specs=in_specs,
        out_specs=out_specs,
        out_shape=out_shape,
        compiler_params=_cparams(("arbitrary",)),
        name="attn_proj",
    )(x, w_bf, *prev_kv)


def _log_sigmoid(x):
    return jnp.minimum(x, 0.0) - jnp.log1p(jnp.exp(-jnp.abs(x)))


def _gla_proj_kernel(x_ref, w_ref, wr_ref, wup_ref, bg_ref, q_ref, k_ref, v_ref, g_ref, la_ref):
    x = x_ref[...].astype(BF16)

    def cols(a, b):
        return jnp.dot(x, w_ref[:, a:b], preferred_element_type=F32)

    q_ref[...] = cols(0, G_QK) * (G_DK ** -0.5)
    k_ref[...] = cols(G_QK, 2 * G_QK)
    v_ref[...] = cols(2 * G_QK, 2 * G_QK + G_V)
    g_ref[...] = cols(2 * G_QK + G_V, 2 * G_QK + 2 * G_V)
    r = jnp.dot(x, wr_ref[...], preferred_element_type=F32)
    gate = jnp.dot(r.astype(BF16), wup_ref[...], preferred_element_type=F32) + bg_ref[...]
    la_ref[...] = _log_sigmoid(gate) * (LOG2E / G_NORMALIZER)


def gla_proj(x, w_bf, wr_bf, wup_bf, b_gate, tm):
    m = x.shape[0]
    row = lambda i: (i, 0)
    full = lambda i: (0, 0)
    return pl.pallas_call(
        _gla_proj_kernel,
        grid=(m // tm,),
        in_specs=[pl.BlockSpec((tm, D_MODEL), row),
                  pl.BlockSpec((D_MODEL, 2 * G_QK + 2 * G_V), full),
                  pl.BlockSpec((D_MODEL, G_RANK_PAD), full),
                  pl.BlockSpec((G_RANK_PAD, G_QK), full),
                  pl.BlockSpec((1, G_QK), full)],
        out_specs=[pl.BlockSpec((tm, G_QK), row), pl.BlockSpec((tm, G_QK), row),
                   pl.BlockSpec((tm, G_V), row), pl.BlockSpec((tm, G_V), row),
                   pl.BlockSpec((tm, G_QK), row)],
        out_shape=[jax.ShapeDtypeStruct((m, G_QK), F32), jax.ShapeDtypeStruct((m, G_QK), F32),
                   jax.ShapeDtypeStruct((m, G_V), F32), jax.ShapeDtypeStruct((m, G_V), F32),
                   jax.ShapeDtypeStruct((m, G_QK), F32)],
        compiler_params=_cparams(("parallel",)),
        name="gla_proj",
    )(x, w_bf, wr_bf, wup_bf, b_gate)


def _finish_kernel(o_ref, g_ref, h_ref, w_ref, nw_ref, lg_ref, lb_ref, out_ref, *, head_dim, scale):
    o = o_ref[...]
    nw = nw_ref[...]
    parts = []
    for c in range(o.shape[1] // head_dim):
        oh = o[:, c * head_dim:(c + 1) * head_dim]
        ms = jnp.mean(oh * oh, axis=-1, keepdims=True)
        parts.append(oh * lax.rsqrt(ms + RMS_EPS) * nw)
    on = jnp.concatenate(parts, axis=1)
    if scale != 1.0:
        on = on * scale
    g = g_ref[...]
    y = on * (g * jax.nn.sigmoid(g))
    y = jnp.dot(y.astype(BF16), w_ref[...], preferred_element_type=F32)
    x = DN_ALPHA * h_ref[...] + y
    mu = jnp.mean(x, axis=-1, keepdims=True)
    xc = x - mu
    var = jnp.mean(xc * xc, axis=-1, keepdims=True)
    out_ref[...] = xc * lax.rsqrt(var + LN_EPS) * lg_ref[...] + lb_ref[...]


def finish(o, g, h, w_bf, norm_w, ln_g, ln_b, head_dim, scale, tm):
    m, n = o.shape
    row = lambda i: (i, 0)
    full = lambda i: (0, 0)
    return pl.pallas_call(
        functools.partial(_finish_kernel, head_dim=head_dim, scale=scale),
        grid=(m // tm,),
        in_specs=[pl.BlockSpec((tm, n), row), pl.BlockSpec((tm, n), row),
                  pl.BlockSpec((tm, D_MODEL), row),
                  pl.BlockSpec((n, D_MODEL), full),
                  pl.BlockSpec((1, head_dim), full),
                  pl.BlockSpec((1, D_MODEL), full), pl.BlockSpec((1, D_MODEL), full)],
        out_specs=pl.BlockSpec((tm, D_MODEL), row),
        out_shape=jax.ShapeDtypeStruct((m, D_MODEL), F32),
        compiler_params=_cparams(("parallel",)),
        name="finish",
    )(o, g, h, w_bf, norm_w.reshape(1, head_dim), ln_g.reshape(1, D_MODEL), ln_b.reshape(1, D_MODEL))


def _lambda_value(lam_ref, lam_init):
    lv = lam_ref[...]
    e1 = jnp.exp(jnp.sum(lv[0:1] * lv[1:2], axis=-1, keepdims=True))
    e2 = jnp.exp(jnp.sum(lv[2:3] * lv[3:4], axis=-1, keepdims=True))
    return e1 - e2 + lam_init


def _split_maps(q):
    lane = lax.broadcasted_iota(jnp.int32, q.shape, 1)
    zero = jnp.zeros_like(q)
    return jnp.concatenate([jnp.where(lane < A_HEAD_DIM, q, zero),
                            jnp.where(lane >= A_HEAD_DIM, q, zero)], axis=0)


def _bf16_terms(x):
    t1 = x.astype(BF16).astype(F32)
    r1 = x - t1
    t2 = r1.astype(BF16).astype(F32)
    t3 = (r1 - t2).astype(BF16).astype(F32)
    return t1, t2, t3


def _flash_kernel(c_ref, lam_ref, q_ref, ka_ref, vt_ref, o_ref, m_sc, acc_sc, sa_sc, ma_sc, sb_sc, mb_sc,
                  *, lam_init, last_rows):
    t = ATT_TILE
    w = ATT_Q
    h = pl.program_id(1)
    qi = pl.program_id(2)
    c = c_ref[h]
    sub = lax.broadcasted_iota(jnp.int32, (KA_COLS - A_HD2, 1), 0)
    coef = jnp.where(sub < 3, c, jnp.where(sub < 6, c * t, 0.0))
    t1, t2, t3 = _bf16_terms(coef)
    part0 = (sub == 0) | (sub == 3)
    part1 = (sub == 1) | (sub == 4)
    coef = jnp.where(part0, t1, jnp.where(part1, t2, t3))

    def attend(nq_rows):
        cols = 2 * nq_rows
        q_t = _split_maps(q_ref[0:nq_rows, :].astype(F32)).T
        qa_t = jnp.concatenate([q_t, jnp.broadcast_to(coef, (KA_COLS - A_HD2, cols))], axis=0).astype(BF16)

        def scores(u):
            start = pl.multiple_of(u * w, w)
            return jnp.dot(ka_ref[pl.ds(start, w), :], qa_t, preferred_element_type=F32)

        def prefetch(u, s_ref, smax_ref):
            s = scores(u)
            s_ref[:, 0:cols] = s
            smax_ref[:, 0:cols] = jnp.max(s, axis=0, keepdims=True)

        def accumulate(s, s_max, u):
            offset = c * (w * (u - qi)).astype(F32)
            m_old = m_sc[:, 0:cols]
            m_rel = jnp.maximum(m_old - offset, s_max)
            m_new = m_rel + offset
            alpha = jnp.exp2(m_old - m_new)
            p = jnp.exp2(s - m_rel).astype(BF16)
            pv = jnp.dot(vt_ref[2 * u], p[0:t], preferred_element_type=F32)
            pv += jnp.dot(vt_ref[2 * u + 1], p[t:w], preferred_element_type=F32)
            acc_sc[:, 0:cols] = alpha * acc_sc[:, 0:cols] + pv
            m_sc[:, 0:cols] = m_new

        def accumulate_diagonal(s):
            key = lax.broadcasted_iota(jnp.int32, (w, cols), 0)
            col = lax.broadcasted_iota(jnp.int32, (w, cols), 1)
            s = jnp.where(key <= jnp.where(col >= nq_rows, col - nq_rows, col), s, MASK_VALUE)
            accumulate(s, jnp.max(s, axis=0, keepdims=True), qi)

        m_sc[...] = jnp.full_like(m_sc, MASK_VALUE)
        acc_sc[...] = jnp.zeros_like(acc_sc)
        prefetch(0, sa_sc, ma_sc)

        def body(jj, carry):
            u = 2 * jj
            prefetch(u + 1, sb_sc, mb_sc)
            accumulate(sa_sc[:, 0:cols], ma_sc[:, 0:cols], u)
            prefetch(u + 2, sa_sc, ma_sc)
            accumulate(sb_sc[:, 0:cols], mb_sc[:, 0:cols], u + 1)
            return carry

        lax.fori_loop(0, qi // 2, body, 0)

        @pl.when(qi % 2 == 1)
        def _():
            s_diag = scores(qi)
            accumulate(sa_sc[:, 0:cols], ma_sc[:, 0:cols], qi - 1)
            accumulate_diagonal(s_diag)

        @pl.when(qi % 2 == 0)
        def _():
            accumulate_diagonal(sa_sc[:, 0:cols])

        lam = _lambda_value(lam_ref, lam_init)
        acc = acc_sc[:, 0:cols]
        o = acc[0:A_HD2] / acc[A_HD2:A_HD2 + 1]
        o_ref[0:nq_rows, :] = (o[:, :nq_rows] - lam * o[:, nq_rows:]).T
        if nq_rows < w:
            o_ref[nq_rows:w, :] = jnp.zeros((w - nq_rows, A_HD2), F32)

    if last_rows == w:
        attend(w)
    else:
        is_last = qi == pl.num_programs(2) - 1
        pl.when(jnp.logical_not(is_last))(lambda: attend(w))
        pl.when(is_last)(lambda: attend(last_rows))


def flash_prompt(q, ka, vt, lam_rows, lam_init, batch, lp, valid_len):
    w = ATT_Q
    nq = lp // w
    lane_tile = A_HD2
    last_rows = min(w, -(-(valid_len - (nq - 1) * w) // lane_tile) * lane_tile)
    coefs = np.array([2.0 ** (-8.0 * (i + 1) / A_HEADS) * LOG2E for i in range(A_HEADS)], np.float32)
    return pl.pallas_call(
        functools.partial(_flash_kernel, lam_init=lam_init, last_rows=last_rows),
        grid=(batch, A_HEADS, nq),
        in_specs=[pl.BlockSpec(memory_space=pltpu.SMEM),
                  pl.BlockSpec((4, A_HEAD_DIM), lambda b, h, i: (0, 0)),
                  pl.BlockSpec((w, A_HD2), lambda b, h, i: (b * nq + i, h)),
                  pl.BlockSpec((None, lp, KA_COLS), lambda b, h, i: (h, b, 0)),
                  pl.BlockSpec((None, None, lp // ATT_TILE, VT_ROWS, ATT_TILE),
                               lambda b, h, i: (b, h, 0, 0, 0))],
        out_specs=pl.BlockSpec((w, A_HD2), lambda b, h, i: (b * nq + i, h)),
        out_shape=jax.ShapeDtypeStruct((batch * lp, A_QK), F32),
        scratch_shapes=[pltpu.VMEM((1, 2 * w), F32), pltpu.VMEM((VT_ROWS, 2 * w), F32),
                        pltpu.VMEM((w, 2 * w), F32), pltpu.VMEM((1, 2 * w), F32),
                        pltpu.VMEM((w, 2 * w), F32), pltpu.VMEM((1, 2 * w), F32)],
        compiler_params=_cparams(("parallel", "parallel", "arbitrary")),
        name="flash_prompt",
    )(jnp.asarray(coefs), lam_rows, q, ka, vt)


def _decode_kernel(pt_ref, lam_ref, q_ref, kn_ref, vn_ref, bias_ref, slope_ref, *rest, lam_init, past_len):
    del pt_ref
    k_refs = rest[:DEC_PAGES]
    v_refs = rest[DEC_PAGES:2 * DEC_PAGES]
    o_ref, m_sc, l_sc, acc_sc = rest[2 * DEC_PAGES:]
    j = pl.program_id(1)
    qq = _split_maps(q_ref[...]).astype(BF16)
    slope_col = slope_ref[...]

    @pl.when(j == 0)
    def _():
        kn = kn_ref[...]
        kn2 = jnp.concatenate([kn, kn], axis=0)
        m_sc[...] = jnp.sum(qq.astype(F32) * kn2, axis=-1, keepdims=True)
        l_sc[...] = jnp.ones_like(l_sc)
        vn = vn_ref[...]
        acc_sc[...] = jnp.concatenate([vn, vn], axis=0)

    bias = bias_ref[...]
    width = PAGE_SIZE * A_HEADS
    s_pages = []
    for i in range(DEC_PAGES):
        page = j * DEC_PAGES + i
        k = k_refs[i][...].reshape(width, A_HD2).astype(BF16)
        offset = slope_col * (page * PAGE_SIZE - past_len).astype(F32)
        s_pages.append(lax.dot_general(qq, k, NT_DIMS, preferred_element_type=F32) + (bias + offset))
    m_old = m_sc[...]
    m_new = m_old
    for s in s_pages:
        m_new = jnp.maximum(m_new, jnp.max(s, axis=-1, keepdims=True))
    alpha = jnp.exp2(m_old - m_new)
    l_new = alpha * l_sc[...]
    acc = alpha * acc_sc[...]
    for i, s in enumerate(s_pages):
        p = jnp.exp2(s - m_new)
        l_new += jnp.sum(p, axis=-1, keepdims=True)
        v = v_refs[i][...].reshape(width, A_HD2).astype(BF16)
        acc += jnp.dot(p.astype(BF16), v, preferred_element_type=F32)
    l_sc[...] = l_new
    acc_sc[...] = acc
    m_sc[...] = m_new

    @pl.when(j == pl.num_programs(1) - 1)
    def _():
        lam = _lambda_value(lam_ref, lam_init)
        o = acc_sc[...] / l_sc[...]
        o_ref[...] = o[:A_HEADS] - lam * o[A_HEADS:]


def decode_attn(q_s, k_s, v_s, cache_k, cache_v, page_table, layer, lam_rows, lam_init):
    db = q_s.shape[0]
    n_pages = page_table.shape[1]
    past_len = n_pages * PAGE_SIZE
    slopes = np.array([2.0 ** (-8.0 * (i + 1) / A_HEADS) * LOG2E for i in range(A_HEADS)], np.float32)
    rows_h = np.tile(np.arange(A_HEADS), 2)
    cols_t = np.repeat(np.arange(PAGE_SIZE), A_HEADS)
    cols_h = np.tile(np.arange(A_HEADS), PAGE_SIZE)
    bias = np.where(rows_h[:, None] == cols_h[None, :],
                    slopes[rows_h][:, None] * cols_t[None, :].astype(np.float32),
                    np.float32(MASK_VALUE)).astype(np.float32)
    slope_col = slopes[rows_h][:, None]

    hd = lambda a: a.reshape(db, A_HEADS, A_HD2)
    per_b = pl.BlockSpec((None, A_HEADS, A_HD2), lambda b, j, pt: (b, 0, 0))
    const2 = lambda b, j, pt: (0, 0)

    def page_spec(i):
        return pl.BlockSpec((None, None, PAGE_SIZE, A_HEADS, A_HD2),
                            lambda b, j, pt: (pt[b, j * DEC_PAGES + i], layer, 0, 0, 0))

    grid_spec = pltpu.PrefetchScalarGridSpec(
        num_scalar_prefetch=1,
        grid=(db, n_pages // DEC_PAGES),
        in_specs=[pl.BlockSpec((4, A_HEAD_DIM), const2), per_b, per_b, per_b,
                  pl.BlockSpec((2 * A_HEADS, PAGE_SIZE * A_HEADS), const2),
                  pl.BlockSpec((2 * A_HEADS, 1), const2)]
                 + [page_spec(i) for i in range(DEC_PAGES)] * 2,
        out_specs=per_b,
        scratch_shapes=[pltpu.VMEM((2 * A_HEADS, 1), F32), pltpu.VMEM((2 * A_HEADS, 1), F32),
                        pltpu.VMEM((2 * A_HEADS, A_HD2), F32)],
    )
    out = pl.pallas_call(
        functools.partial(_decode_kernel, lam_init=lam_init, past_len=past_len),
        grid_spec=grid_spec,
        out_shape=jax.ShapeDtypeStruct((db, A_HEADS, A_HD2), F32),
        compiler_params=_cparams(("parallel", "arbitrary")),
        name="decode_attn",
    )(page_table, lam_rows, hd(q_s.astype(F32)), hd(k_s), hd(v_s), jnp.asarray(bias), jnp.asarray(slope_col),
      *([cache_k] * DEC_PAGES), *([cache_v] * DEC_PAGES))
    return out.reshape(db, A_QK)


G_LEVELS = tuple(G_CHUNK >> (i + 1) for i in range(G_CHUNK.bit_length() - 1))


def _gla_decay_matrix():
    c = G_CHUNK
    r = np.arange(c)[:, None]
    j = np.arange(c)[None, :]
    blocks = [j <= r, j > r]
    for blk in G_LEVELS:
        ref = (r // (2 * blk)) * (2 * blk) + blk
        blocks.append(np.where(r >= ref, (j > ref) & (j <= r), (j > r) & (j <= ref)))
    d = np.concatenate(blocks, axis=0).astype(np.float32)
    return np.concatenate([d, d, d], axis=1)


def _gla_level_masks():
    c = G_CHUNK
    n = G_HEADS * c
    row = lax.broadcasted_iota(jnp.int32, (c, 1), 0)
    t_i = lax.broadcasted_iota(jnp.int32, (n, n), 0)
    s_i = lax.broadcasted_iota(jnp.int32, (n, n), 1)
    later, owns = [], []
    for blk in G_LEVELS:
        shift = blk.bit_length()
        later.append((row & blk) != 0)
        owns.append(((t_i >> shift) == (s_i >> shift)) & ((t_i & blk) != 0) & ((s_i & blk) == 0))
    return later, owns, t_i == s_i


def _heads_to_rows(x, width):
    return jnp.concatenate([x[:, h * width:(h + 1) * width] for h in range(G_HEADS)], axis=0)


def _gla_chunk(q, k, v, la, states, dmat, masks):
    c = q.shape[0]
    later, owns, diagonal = masks
    terms = jnp.concatenate(_bf16_terms(la), axis=0).astype(BF16)
    e = jnp.dot(dmat, terms, preferred_element_type=F32)
    b = e[0:c]
    vb = v.astype(BF16)

    qe = (q * jnp.exp2(b)).astype(BF16)
    qk = _heads_to_rows(q * k, G_DK)
    a = jnp.where(diagonal, jnp.sum(qk, axis=-1, keepdims=True), 0.0)
    for i in range(len(G_LEVELS)):
        x = (jnp.where(later[i], q, k) * jnp.exp2(e[(2 + i) * c:(3 + i) * c])).astype(BF16)
        xs = _heads_to_rows(x, G_DK)
        a = jnp.where(owns[i], lax.dot_general(xs, xs, NT_DIMS, preferred_element_type=F32), a)
    o_intra = jnp.dot(a.astype(BF16), _heads_to_rows(vb, G_DV), preferred_element_type=F32)

    k_dec = (k * jnp.exp2(e[c:2 * c])).astype(BF16)
    decay = jnp.exp2(b[c - 1:c])
    outs, new_states = [], []
    for h in range(G_HEADS):
        ks = slice(h * G_DK, (h + 1) * G_DK)
        st = states[h]
        o_inter = lax.dot_general(qe[:, ks], st.astype(BF16), NT_DIMS, preferred_element_type=F32)
        outs.append(o_inter + o_intra[h * c:(h + 1) * c])
        new_states.append(st * decay[:, ks] + lax.dot_general(
            vb[:, h * G_DV:(h + 1) * G_DV], k_dec[:, ks], TN_DIMS, preferred_element_type=F32))
    return outs, new_states


def _gla_kernel(dmat_ref, q_ref, k_ref, v_ref, la_ref, o_ref, s_ref, st_sc, *, valid_len):
    i = pl.program_id(1)

    @pl.when(i == 0)
    def _():
        st_sc[...] = jnp.zeros_like(st_sc)

    has_real_rows = i * GLA_ROWS < valid_len

    @pl.when(has_real_rows)
    def _():
        dmat = dmat_ref[...]
        masks = _gla_level_masks()
        states = [st_sc[h] for h in range(G_HEADS)]
        for c in range(GLA_ROWS // G_CHUNK):
            sl = slice(c * G_CHUNK, (c + 1) * G_CHUNK)
            pos = i * GLA_ROWS + c * G_CHUNK + lax.broadcasted_iota(jnp.int32, (G_CHUNK, 1), 0)
            valid = pos < valid_len
            la = jnp.where(valid, la_ref[sl, :], 0.0)
            k = jnp.where(valid, k_ref[sl, :], 0.0)
            outs, states = _gla_chunk(q_ref[sl, :], k, v_ref[sl, :], la, states, dmat, masks)
            for h in range(G_HEADS):
                o_ref[sl, h * G_DV:(h + 1) * G_DV] = outs[h]
        for h in range(G_HEADS):
            st_sc[h] = states[h]

    @pl.when(jnp.logical_not(has_real_rows))
    def _():
        o_ref[...] = jnp.zeros_like(o_ref)

    @pl.when(i == pl.num_programs(1) - 1)
    def _():
        for h in range(G_HEADS):
            s_ref[h] = st_sc[h].T


def gla_prompt(q, k, v, la, batch, lp, valid_len):
    nblk = lp // GLA_ROWS
    qk_spec = pl.BlockSpec((GLA_ROWS, G_QK), lambda b, i: (b * nblk + i, 0))
    v_spec = pl.BlockSpec((GLA_ROWS, G_V), lambda b, i: (b * nblk + i, 0))
    dmat = jnp.asarray(_gla_decay_matrix(), BF16)
    return pl.pallas_call(
        functools.partial(_gla_kernel, valid_len=valid_len),
        grid=(batch, nblk),
        in_specs=[pl.BlockSpec(dmat.shape, lambda b, i: (0, 0)), qk_spec, qk_spec, v_spec, qk_spec],
        out_specs=[v_spec, pl.BlockSpec((None, G_HEADS, G_DK, G_DV), lambda b, i: (b, 0, 0, 0))],
        out_shape=[jax.ShapeDtypeStruct((batch * lp, G_V), F32),
                   jax.ShapeDtypeStruct((batch, G_HEADS, G_DK, G_DV), F32)],
        scratch_shapes=[pltpu.VMEM((G_HEADS, G_DV, G_DK), F32)],
        compiler_params=_cparams(("parallel", "arbitrary")),
        name="gla_prompt",
    )(dmat, q, k, v, la)


def _stack_rows(rows, n_rows=16):
    n = rows[0].shape[1]
    idx = lax.broadcasted_iota(jnp.int32, (n_rows, n), 0)
    out = jnp.zeros((n_rows, n), F32)
    for r, x in enumerate(rows):
        out = jnp.where(idx == r, x, out)
    return out.astype(BF16)


def _gla_step_kernel(q_ref, k_ref, v_ref, la_ref, s_ref, o_ref, so_ref):
    q = q_ref[...]
    k = k_ref[...]
    v = v_ref[...]
    a = jnp.exp2(la_ref[...])
    ones = jnp.ones((16, G_DV), BF16)
    o_parts = []
    for h in range(G_HEADS):
        ks = slice(h * G_DK, (h + 1) * G_DK)
        vs = slice(h * G_DV, (h + 1) * G_DV)
        a_h = a[:, ks]
        a1 = a_h.astype(BF16)
        r1 = a_h - a1.astype(F32)
        a2 = r1.astype(BF16)
        a3 = (r1 - a2.astype(F32)).astype(BF16)
        a_rows = _stack_rows([a1.astype(F32), a2.astype(F32), a3.astype(F32)])
        a_col = lax.dot_general(a_rows, ones, TN_DIMS, preferred_element_type=F32)
        kv = lax.dot_general(_stack_rows([k[:, ks]]), _stack_rows([v[:, vs]]), TN_DIMS,
                             preferred_element_type=F32)
        s1 = a_col * s_ref[h] + kv
        so_ref[h] = s1
        o = jnp.dot(_stack_rows([q[:, ks]]), s1.astype(BF16), preferred_element_type=F32)
        o_parts.append(o[0:1])
    o_ref[...] = jnp.concatenate(o_parts, axis=1)


def gla_step(q_s, k_s, v_s, la_s, state_gla, layer):
    db = q_s.shape[0]
    r3 = lambda a: a.reshape(db, 1, a.shape[1])
    vec = lambda n: pl.BlockSpec((None, 1, n), lambda b: (b, 0, 0))
    o, s_new = pl.pallas_call(
        _gla_step_kernel,
        grid=(db,),
        in_specs=[vec(G_QK), vec(G_QK), vec(G_V), vec(G_QK),
                  pl.BlockSpec((None, None, G_HEADS, G_DK, G_DV), lambda b: (b, layer, 0, 0, 0))],
        out_specs=[vec(G_V), pl.BlockSpec((None, G_HEADS, G_DK, G_DV), lambda b: (b, 0, 0, 0))],
        out_shape=[jax.ShapeDtypeStruct((db, 1, G_V), F32),
                   jax.ShapeDtypeStruct((db, G_HEADS, G_DK, G_DV), F32)],
        compiler_params=_cparams(("parallel",)),
        name="gla_step",
    )(r3(q_s), r3(k_s), r3(v_s), r3(la_s), state_gla)
    return o.reshape(db, G_V), s_new


def kernel(x_prompt, x_sample, cache_k, cache_v, state_gla, page_table, meta_tokens, attn_w_in, attn_lq1, attn_lk1, attn_lq2, attn_lk2, attn_subln_w, attn_w_out, gla_w_in, gla_w_gate_up, gla_b_gate, gla_norm_w, gla_w_out, ln_g, ln_b):
    batch, seq, _ = x_prompt.shape
    db = x_sample.shape[0]
    length = N_META + seq
    lp = -(-length // ATT_Q) * ATT_Q
    meta = jnp.broadcast_to(meta_tokens[None].astype(x_prompt.dtype), (batch, N_META, D_MODEL))
    h_p = lax.pad(x_prompt, jnp.zeros((), x_prompt.dtype), ((0, 0, 0), (N_META, lp - length, 0), (0, 0, 0)))
    h_p = lax.dynamic_update_slice(h_p, meta, (0, 0, 0)).reshape(batch * lp, D_MODEL)
    h_s = x_sample.reshape(db, D_MODEL)

    assert DEPTH == 4
    kv_p = ()
    k_s_rows, v_s_rows, s_p_list, s_s_list = [], [], [], []
    for i in range(DEPTH):
        if i % 2 == 0:
            a = i // 2
            lam_init = 0.8 - 0.6 * math.exp(-0.3 * i)
            w_in = attn_w_in[a].astype(BF16)
            w_out = attn_w_out[a].astype(BF16)
            lam_rows = jnp.stack([attn_lq1[a], attn_lk1[a], attn_lq2[a], attn_lk2[a]]).astype(F32)
            q_p, k_p, v_p, g_p, ka_p, vt_p = attn_proj(h_p, w_in, ATT_TILE, batch, length, kv_p)
            kv_p = (k_p, v_p)
            q_s, k_s, v_s, g_s = attn_proj(h_s, w_in, db)
            o_p = flash_prompt(q_p, ka_p, vt_p, lam_rows, lam_init, batch, lp, length)
            o_s = decode_attn(q_s, k_s, v_s, cache_k, cache_v, page_table, a, lam_rows, lam_init)
            fin = functools.partial(finish, w_bf=w_out, norm_w=attn_subln_w[a], ln_g=ln_g[i], ln_b=ln_b[i],
                                    head_dim=A_HD2, scale=1.0 - lam_init)
            h_p = fin(o_p, g_p, h_p, tm=PROJ_ROWS)
            h_s = fin(o_s, g_s, h_s, tm=db)
            k_s_rows.append(k_s.reshape(db, 1, A_HEADS, A_HD2))
            v_s_rows.append(v_s.reshape(db, 1, A_HEADS, A_HD2))
        else:
            gi = i // 2
            extra = G_RANK_PAD - G_RANK
            n_main = 2 * G_QK + 2 * G_V
            w_in = gla_w_in[gi][:, :n_main].astype(BF16)
            w_rank = jnp.pad(gla_w_in[gi][:, n_main:], ((0, 0), (0, extra))).astype(BF16)
            w_up = jnp.pad(gla_w_gate_up[gi], ((0, extra), (0, 0))).astype(BF16)
            w_out = gla_w_out[gi].astype(BF16)
            b_gate = gla_b_gate[gi].reshape(1, G_QK).astype(F32)
            q_p, k_p, v_p, g_p, la_p = gla_proj(h_p, w_in, w_rank, w_up, b_gate, PROJ_ROWS)
            q_s, k_s, v_s, g_s, la_s = gla_proj(h_s, w_in, w_rank, w_up, b_gate, db)
            o_p, s_p = gla_prompt(q_p, k_p, v_p, la_p, batch, lp, length)
            o_s, s_s = gla_step(q_s, k_s, v_s, la_s, state_gla, gi)
            fin = functools.partial(finish, w_bf=w_out, norm_w=gla_norm_w[gi], ln_g=ln_g[i], ln_b=ln_b[i],
                                    head_dim=G_DV, scale=1.0)
            h_p = fin(o_p, g_p, h_p, tm=PROJ_ROWS)
            h_s = fin(o_s, g_s, h_s, tm=db)
            s_p_list.append(s_p)
            s_s_list.append(s_s)

    y_prompt = h_p.reshape(batch, lp, D_MODEL)[:, N_META:length]
    kv_shape = (batch, DEPTH // 2, length, A_HEADS, A_HD2)
    return (y_prompt, h_s.reshape(db, 1, D_MODEL),
            kv_p[0].reshape(kv_shape), kv_p[1].reshape(kv_shape), jnp.stack(s_p_list, axis=1),
            jnp.stack(k_s_rows, axis=1), jnp.stack(v_s_rows, axis=1), jnp.stack(s_s_list, axis=1))
```

```python
import functools
import math

import numpy as np
import jax
import jax.numpy as jnp
from jax import lax
from jax.experimental import pallas as pl
from jax.experimental.pallas import tpu as pltpu

F32 = jnp.float32
BF16 = jnp.bfloat16

D_MODEL = 1024
DEPTH = 4
N_META = 16
PAGE_SIZE = 128
A_HEADS = 8
A_HEAD_DIM = 64
A_HD2 = 2 * A_HEAD_DIM
A_QK = A_HEADS * A_HD2
G_HEADS = 4
G_DK = 128
G_DV = 256
G_QK = G_HEADS * G_DK
G_V = G_HEADS * G_DV
G_RANK = 16
G_RANK_PAD = 128
G_NORMALIZER = 16.0
G_CHUNK = 64
DN_ALPHA = (2.0 * DEPTH) ** 0.25
LN_EPS = 1e-5
RMS_EPS = 1e-5

MASK_VALUE = -1e30
LOG2E = math.log2(math.e)
ATT_TILE = 256
ATT_Q = 2 * ATT_TILE
VT_ROWS = A_HD2 + 16
KA_COLS = A_HD2 + 16
GLA_ROWS = 512
PROJ_ROWS = 512
FINISH_ROWS = 1024
DEC_PAGES = 8
VMEM_LIMIT = 56 * 1024 * 1024

NT_DIMS = (((1,), (1,)), ((), ()))
TN_DIMS = (((0,), (0,)), ((), ()))


def _cparams(sem):
    return pltpu.CompilerParams(dimension_semantics=sem, vmem_limit_bytes=VMEM_LIMIT)


def _proj_tile(step, tiles_per_seq, last_kv_tile):
    return jnp.where(step >= last_kv_tile, tiles_per_seq - 1 + last_kv_tile - step, step)


def _attn_proj_kernel(x_ref, w_ref, *refs, tiles_per_seq, last_kv_tile, prev_kv):
    n_in = 2 if prev_kv else 0
    q_ref, k_ref, v_ref, g_ref = refs[n_in:n_in + 4]
    flash_refs = refs[:n_in] + refs[n_in + 4:]
    x = x_ref[...].astype(BF16)

    def cols(c):
        return jnp.dot(x, w_ref[:, c * A_QK:(c + 1) * A_QK], preferred_element_type=F32)

    q_ref[...] = (cols(0) * (A_HEAD_DIM ** -0.5 * LOG2E)).astype(BF16)
    k = cols(1)
    v = cols(2)
    g_ref[...] = cols(3)
    if tiles_per_seq is None:
        k_ref[...] = k
        v_ref[...] = v
        return
    if prev_kv:
        kprev_ref, vprev_ref, ka_ref, vt_ref = flash_refs
        k_ref[0] = kprev_ref[...]
        v_ref[0] = vprev_ref[...]
        k_ref[1] = k
        v_ref[1] = v
    else:
        ka_ref, vt_ref = flash_refs
        k_ref[...] = k
        v_ref[...] = v
    t = ATT_TILE
    tile = _proj_tile(pl.program_id(0) % tiles_per_seq, tiles_per_seq, last_kv_tile)
    pos_lo = lax.broadcasted_iota(jnp.int32, (t, KA_COLS - A_HD2), 0).astype(F32)
    lane = lax.broadcasted_iota(jnp.int32, (t, KA_COLS - A_HD2), 1)
    parity = (tile % 2).astype(F32)
    pos_cols = jnp.where(lane < 3, pos_lo, jnp.where(lane < 6, parity, 0.0)).astype(BF16)
    sub = lax.broadcasted_iota(jnp.int32, (VT_ROWS - A_HD2, t), 0)
    ones_rows = jnp.where(sub == 0, 1.0, 0.0).astype(BF16)
    for h in range(A_HEADS):
        hs = slice(h * A_HD2, (h + 1) * A_HD2)
        ka_ref[h, :, 0:A_HD2] = k[:, hs].astype(BF16)
        ka_ref[h, :, A_HD2:KA_COLS] = pos_cols
        vt_ref[h, 0:A_HD2, :] = v[:, hs].T.astype(BF16)
        vt_ref[h, A_HD2:VT_ROWS, :] = ones_rows


def attn_proj(x, w_bf, tm, batch=None, valid_len=None, prev_kv=()):
    m = x.shape[0]
    full = lambda i: (0, 0)
    in_specs = [None, pl.BlockSpec((D_MODEL, 4 * A_QK), full)]
    if batch is None:
        tiles_per_seq = last_kv_tile = None
        row = lambda i: (i, 0)
        blk = pl.BlockSpec((tm, A_QK), row)
        sds = lambda dt: jax.ShapeDtypeStruct((m, A_QK), dt)
        out_specs = [blk] * 4
        out_shape = [sds(BF16), sds(F32), sds(F32), sds(F32)]
    else:
        assert tm == ATT_TILE
        tiles_per_seq = m // batch // tm
        last_kv_tile = (valid_len - 1) // tm
        seq = lambda i: i // tiles_per_seq
        tile = lambda i: _proj_tile(i % tiles_per_seq, tiles_per_seq, last_kv_tile)
        row = lambda i: (seq(i) * tiles_per_seq + tile(i), 0)
        blk = pl.BlockSpec((tm, A_QK), row)
        sds = lambda dt: jax.ShapeDtypeStruct((m, A_QK), dt)
        kv_tile = lambda i: jnp.minimum(tile(i), last_kv_tile)
        kv_in = pl.BlockSpec((None, tm, A_QK), lambda i: (seq(i), kv_tile(i), 0))
        if prev_kv:
            kv_out = pl.BlockSpec((None, 2, tm, A_QK), lambda i: (seq(i), 0, kv_tile(i), 0))
            kv_sds = jax.ShapeDtypeStruct((batch, 2, valid_len, A_QK), F32)
            in_specs += [kv_in, kv_in]
        else:
            kv_out = kv_in
            kv_sds = jax.ShapeDtypeStruct((batch, valid_len, A_QK), F32)
        out_specs = [blk, kv_out, kv_out, blk,
                     pl.BlockSpec((A_HEADS, tm, KA_COLS), lambda i: (0, row(i)[0], 0)),
                     pl.BlockSpec((None, A_HEADS, None, VT_ROWS, tm), lambda i: (seq(i), 0, tile(i), 0, 0))]
        out_shape = [sds(BF16), kv_sds, kv_sds, sds(F32),
                     jax.ShapeDtypeStruct((A_HEADS, m, KA_COLS), BF16),
                     jax.ShapeDtypeStruct((batch, A_HEADS, tiles_per_seq, VT_ROWS, tm), BF16)]
    in_specs[0] = pl.BlockSpec((tm, D_MODEL), row)
    return pl.pallas_call(
        functools.partial(_attn_proj_kernel, tiles_per_seq=tiles_per_seq, last_kv_tile=last_kv_tile,
                          prev_kv=bool(prev_kv)),
        grid=(m // tm,),
        in_specs=in_specs,
        out_specs=out_specs,
        out_shape=out_shape,
        compiler_params=_cparams(("arbitrary",)),
        name="attn_proj",
    )(x, w_bf, *prev_kv)


def _log_sigmoid(x):
    return jnp.minimum(x, 0.0) - jnp.log1p(jnp.exp(-jnp.abs(x)))


def _gla_proj_kernel(x_ref, w_ref, wr_ref, wup_ref, bg_ref, q_ref, k_ref, v_ref, g_ref, la_ref):
    x = x_ref[...].astype(BF16)

    def cols(a, b):
        return jnp.dot(x, w_ref[:, a:b], preferred_element_type=F32)

    q_ref[...] = cols(0, G_QK) * (G_DK ** -0.5)
    k_ref[...] = cols(G_QK, 2 * G_QK)
    v_ref[...] = cols(2 * G_QK, 2 * G_QK + G_V)
    g_ref[...] = cols(2 * G_QK + G_V, 2 * G_QK + 2 * G_V)
    r = jnp.dot(x, wr_ref[...], preferred_element_type=F32)
    gate = jnp.dot(r.astype(BF16), wup_ref[...], preferred_element_type=F32) + bg_ref[...]
    la_ref[...] = _log_sigmoid(gate) * (LOG2E / G_NORMALIZER)


def gla_proj(x, w_bf, wr_bf, wup_bf, b_gate, tm):
    m = x.shape[0]
    row = lambda i: (i, 0)
    full = lambda i: (0, 0)
    return pl.pallas_call(
        _gla_proj_kernel,
        grid=(m // tm,),
        in_specs=[pl.BlockSpec((tm, D_MODEL), row),
                  pl.BlockSpec((D_MODEL, 2 * G_QK + 2 * G_V), full),
                  pl.BlockSpec((D_MODEL, G_RANK_PAD), full),
                  pl.BlockSpec((G_RANK_PAD, G_QK), full),
                  pl.BlockSpec((1, G_QK), full)],
        out_specs=[pl.BlockSpec((tm, G_QK), row), pl.BlockSpec((tm, G_QK), row),
                   pl.BlockSpec((tm, G_V), row), pl.BlockSpec((tm, G_V), row),
                   pl.BlockSpec((tm, G_QK), row)],
        out_shape=[jax.ShapeDtypeStruct((m, G_QK), F32), jax.ShapeDtypeStruct((m, G_QK), F32),
                   jax.ShapeDtypeStruct((m, G_V), F32), jax.ShapeDtypeStruct((m, G_V), F32),
                   jax.ShapeDtypeStruct((m, G_QK), F32)],
        compiler_params=_cparams(("parallel",)),
        name="gla_proj",
    )(x, w_bf, wr_bf, wup_bf, b_gate)


def _finish_kernel(o_ref, g_ref, h_ref, w_ref, nw_ref, lg_ref, lb_ref, out_ref, *, head_dim, scale):
    o = o_ref[...]
    nw = nw_ref[...]
    parts = []
    for c in range(o.shape[1] // head_dim):
        oh = o[:, c * head_dim:(c + 1) * head_dim]
        ms = jnp.mean(oh * oh, axis=-1, keepdims=True)
        parts.append(oh * lax.rsqrt(ms + RMS_EPS) * nw)
    on = jnp.concatenate(parts, axis=1)
    if scale != 1.0:
        on = on * scale
    g = g_ref[...]
    y = on * (g * jax.nn.sigmoid(g))
    y = jnp.dot(y.astype(BF16), w_ref[...], preferred_element_type=F32)
    x = DN_ALPHA * h_ref[...] + y
    mu = jnp.mean(x, axis=-1, keepdims=True)
    xc = x - mu
    var = jnp.mean(xc * xc, axis=-1, keepdims=True)
    out_ref[...] = xc * lax.rsqrt(var + LN_EPS) * lg_ref[...] + lb_ref[...]


def finish(o, g, h, w_bf, norm_w, ln_g, ln_b, head_dim, scale, tm):
    m, n = o.shape
    row = lambda i: (i, 0)
    full = lambda i: (0, 0)
    return pl.pallas_call(
        functools.partial(_finish_kernel, head_dim=head_dim, scale=scale),
        grid=(m // tm,),
        in_specs=[pl.BlockSpec((tm, n), row), pl.BlockSpec((tm, n), row),
                  pl.BlockSpec((tm, D_MODEL), row),
                  pl.BlockSpec((n, D_MODEL), full),
                  pl.BlockSpec((1, head_dim), full),
                  pl.BlockSpec((1, D_MODEL), full), pl.BlockSpec((1, D_MODEL), full)],
        out_specs=pl.BlockSpec((tm, D_MODEL), row),
        out_shape=jax.ShapeDtypeStruct((m, D_MODEL), F32),
        compiler_params=_cparams(("parallel",)),
        name="finish",
    )(o, g, h, w_bf, norm_w.reshape(1, head_dim), ln_g.reshape(1, D_MODEL), ln_b.reshape(1, D_MODEL))


def _lambda_value(lam_ref, lam_init):
    lv = lam_ref[...]
    e1 = jnp.exp(jnp.sum(lv[0:1] * lv[1:2], axis=-1, keepdims=True))
    e2 = jnp.exp(jnp.sum(lv[2:3] * lv[3:4], axis=-1, keepdims=True))
    return e1 - e2 + lam_init


def _split_maps(q):
    lane = lax.broadcasted_iota(jnp.int32, q.shape, 1)
    zero = jnp.zeros_like(q)
    return jnp.concatenate([jnp.where(lane < A_HEAD_DIM, q, zero),
                            jnp.where(lane >= A_HEAD_DIM, q, zero)], axis=0)


def _bf16_terms(x):
    t1 = x.astype(BF16).astype(F32)
    r1 = x - t1
    t2 = r1.astype(BF16).astype(F32)
    t3 = (r1 - t2).astype(BF16).astype(F32)
    return t1, t2, t3


def _flash_kernel(c_ref, lam_ref, q_ref, ka_ref, vt_ref, o_ref, m_sc, acc_sc, sa_sc, ma_sc, sb_sc, mb_sc,
                  *, lam_init, last_rows):
    t = ATT_TILE
    w = ATT_Q
    h = pl.program_id(1)
    qi = pl.program_id(2)
    c = c_ref[h]
    sub = lax.broadcasted_iota(jnp.int32, (KA_COLS - A_HD2, 1), 0)
    coef = jnp.where(sub < 3, c, jnp.where(sub < 6, c * t, 0.0))
    t1, t2, t3 = _bf16_terms(coef)
    part0 = (sub == 0) | (sub == 3)
    part1 = (sub == 1) | (sub == 4)
    coef = jnp.where(part0, t1, jnp.where(part1, t2, t3))

    def attend(nq_rows):
        cols = 2 * nq_rows
        q_t = _split_maps(q_ref[0:nq_rows, :].astype(F32)).T
        qa_t = jnp.concatenate([q_t, jnp.broadcast_to(coef, (KA_COLS - A_HD2, cols))], axis=0).astype(BF16)

        def scores(u):
            start = pl.multiple_of(u * w, w)
            return jnp.dot(ka_ref[pl.ds(start, w), :], qa_t, preferred_element_type=F32)

        def prefetch(u, s_ref, smax_ref):
            s = scores(u)
            s_ref[:, 0:cols] = s
            smax_ref[:, 0:cols] = jnp.max(s, axis=0, keepdims=True)

        def accumulate(s, s_max, u):
            offset = c * (w * (u - qi)).astype(F32)
            m_old = m_sc[:, 0:cols]
            m_rel = jnp.maximum(m_old - offset, s_max)
            m_new = m_rel + offset
            alpha = jnp.exp2(m_old - m_new)
            p = jnp.exp2(s - m_rel).astype(BF16)
            pv = jnp.dot(vt_ref[2 * u], p[0:t], preferred_element_type=F32)
            pv += jnp.dot(vt_ref[2 * u + 1], p[t:w], preferred_element_type=F32)
            acc_sc[:, 0:cols] = alpha * acc_sc[:, 0:cols] + pv
            m_sc[:, 0:cols] = m_new

        def accumulate_diagonal(s):
            key = lax.broadcasted_iota(jnp.int32, (w, cols), 0)
            col = lax.broadcasted_iota(jnp.int32, (w, cols), 1)
            s = jnp.where(key <= jnp.where(col >= nq_rows, col - nq_rows, col), s, MASK_VALUE)
            accumulate(s, jnp.max(s, axis=0, keepdims=True), qi)

        m_sc[...] = jnp.full_like(m_sc, MASK_VALUE)
        acc_sc[...] = jnp.zeros_like(acc_sc)
        prefetch(0, sa_sc, ma_sc)

        def body(jj, carry):
            u = 2 * jj
            prefetch(u + 1, sb_sc, mb_sc)
            accumulate(sa_sc[:, 0:cols], ma_sc[:, 0:cols], u)
            prefetch(u + 2, sa_sc, ma_sc)
            accumulate(sb_sc[:, 0:cols], mb_sc[:, 0:cols], u + 1)
            return carry

        lax.fori_loop(0, qi // 2, body, 0)

        @pl.when(qi % 2 == 1)
        def _():
            s_diag = scores(qi)
            accumulate(sa_sc[:, 0:cols], ma_sc[:, 0:cols], qi - 1)
            accumulate_diagonal(s_diag)

        @pl.when(qi % 2 == 0)
        def _():
            accumulate_diagonal(sa_sc[:, 0:cols])

        lam = _lambda_value(lam_ref, lam_init)
        acc = acc_sc[:, 0:cols]
        o = acc[0:A_HD2] / acc[A_HD2:A_HD2 + 1]
        o_ref[0:nq_rows, :] = (o[:, :nq_rows] - lam * o[:, nq_rows:]).T
        if nq_rows < w:
            o_ref[nq_rows:w, :] = jnp.zeros((w - nq_rows, A_HD2), F32)

    if last_rows == w:
        attend(w)
    else:
        is_last = qi == pl.num_programs(2) - 1
        pl.when(jnp.logical_not(is_last))(lambda: attend(w))
        pl.when(is_last)(lambda: attend(last_rows))


def flash_prompt(q, ka, vt, lam_rows, lam_init, batch, lp, valid_len):
    w = ATT_Q
    nq = lp // w
    lane_tile = A_HD2
    last_rows = min(w, -(-(valid_len - (nq - 1) * w) // lane_tile) * lane_tile)
    coefs = np.array([2.0 ** (-8.0 * (i + 1) / A_HEADS) * LOG2E for i in range(A_HEADS)], np.float32)
    return pl.pallas_call(
        functools.partial(_flash_kernel, lam_init=lam_init, last_rows=last_rows),
        grid=(batch, A_HEADS, nq),
        in_specs=[pl.BlockSpec(memory_space=pltpu.SMEM),
                  pl.BlockSpec((4, A_HEAD_DIM), lambda b, h, i: (0, 0)),
                  pl.BlockSpec((w, A_HD2), lambda b, h, i: (b * nq + i, h)),
                  pl.BlockSpec((None, lp, KA_COLS), lambda b, h, i: (h, b, 0)),
                  pl.BlockSpec((None, None, lp // ATT_TILE, VT_ROWS, ATT_TILE),
                               lambda b, h, i: (b, h, 0, 0, 0))],
        out_specs=pl.BlockSpec((w, A_HD2), lambda b, h, i: (b * nq + i, h)),
        out_shape=jax.ShapeDtypeStruct((batch * lp, A_QK), F32),
        scratch_shapes=[pltpu.VMEM((1, 2 * w), F32), pltpu.VMEM((VT_ROWS, 2 * w), F32),
                        pltpu.VMEM((w, 2 * w), F32), pltpu.VMEM((1, 2 * w), F32),
                        pltpu.VMEM((w, 2 * w), F32), pltpu.VMEM((1, 2 * w), F32)],
        compiler_params=_cparams(("parallel", "parallel", "arbitrary")),
        name="flash_prompt",
    )(jnp.asarray(coefs), lam_rows, q, ka, vt)


def _decode_kernel(pt_ref, lam_ref, q_ref, kn_ref, vn_ref, bias_ref, slope_ref, *rest, lam_init, past_len):
    del pt_ref
    k_refs = rest[:DEC_PAGES]
    v_refs = rest[DEC_PAGES:2 * DEC_PAGES]
    o_ref, m_sc, l_sc, acc_sc = rest[2 * DEC_PAGES:]
    j = pl.program_id(1)
    qq = _split_maps(q_ref[...]).astype(BF16)
    slope_col = slope_ref[...]

    @pl.when(j == 0)
    def _():
        kn = kn_ref[...]
        kn2 = jnp.concatenate([kn, kn], axis=0)
        m_sc[...] = jnp.sum(qq.astype(F32) * kn2, axis=-1, keepdims=True)
        l_sc[...] = jnp.ones_like(l_sc)
        vn = vn_ref[...]
        acc_sc[...] = jnp.concatenate([vn, vn], axis=0)

    bias = bias_ref[...]
    width = PAGE_SIZE * A_HEADS
    s_pages = []
    for i in range(DEC_PAGES):
        page = j * DEC_PAGES + i
        k = k_refs[i][...].reshape(width, A_HD2).astype(BF16)
        offset = slope_col * (page * PAGE_SIZE - past_len).astype(F32)
        s_pages.append(lax.dot_general(qq, k, NT_DIMS, preferred_element_type=F32) + (bias + offset))
    m_old = m_sc[...]
    m_new = m_old
    for s in s_pages:
        m_new = jnp.maximum(m_new, jnp.max(s, axis=-1, keepdims=True))
    alpha = jnp.exp2(m_old - m_new)
    l_new = alpha * l_sc[...]
    acc = alpha * acc_sc[...]
    for i, s in enumerate(s_pages):
        p = jnp.exp2(s - m_new)
        l_new += jnp.sum(p, axis=-1, keepdims=True)
        v = v_refs[i][...].reshape(width, A_HD2).astype(BF16)
        acc += jnp.dot(p.astype(BF16), v, preferred_element_type=F32)
    l_sc[...] = l_new
    acc_sc[...] = acc
    m_sc[...] = m_new

    @pl.when(j == pl.num_programs(1) - 1)
    def _():
        lam = _lambda_value(lam_ref, lam_init)
        o = acc_sc[...] / l_sc[...]
        o_ref[...] = o[:A_HEADS] - lam * o[A_HEADS:]


def decode_attn(q_s, k_s, v_s, cache_k, cache_v, page_table, layer, lam_rows, lam_init):
    db = q_s.shape[0]
    n_pages = page_table.shape[1]
    past_len = n_pages * PAGE_SIZE
    slopes = np.array([2.0 ** (-8.0 * (i + 1) / A_HEADS) * LOG2E for i in range(A_HEADS)], np.float32)
    rows_h = np.tile(np.arange(A_HEADS), 2)
    cols_t = np.repeat(np.arange(PAGE_SIZE), A_HEADS)
    cols_h = np.tile(np.arange(A_HEADS), PAGE_SIZE)
    bias = np.where(rows_h[:, None] == cols_h[None, :],
                    slopes[rows_h][:, None] * cols_t[None, :].astype(np.float32),
                    np.float32(MASK_VALUE)).astype(np.float32)
    slope_col = slopes[rows_h][:, None]

    hd = lambda a: a.reshape(db, A_HEADS, A_HD2)
    per_b = pl.BlockSpec((None, A_HEADS, A_HD2), lambda b, j, pt: (b, 0, 0))
    const2 = lambda b, j, pt: (0, 0)

    def page_spec(i):
        return pl.BlockSpec((None, None, PAGE_SIZE, A_HEADS, A_HD2),
                            lambda b, j, pt: (pt[b, j * DEC_PAGES + i], layer, 0, 0, 0))

    grid_spec = pltpu.PrefetchScalarGridSpec(
        num_scalar_prefetch=1,
        grid=(db, n_pages // DEC_PAGES),
        in_specs=[pl.BlockSpec((4, A_HEAD_DIM), const2), per_b, per_b, per_b,
                  pl.BlockSpec((2 * A_HEADS, PAGE_SIZE * A_HEADS), const2),
                  pl.BlockSpec((2 * A_HEADS, 1), const2)]
                 + [page_spec(i) for i in range(DEC_PAGES)] * 2,
        out_specs=per_b,
        scratch_shapes=[pltpu.VMEM((2 * A_HEADS, 1), F32), pltpu.VMEM((2 * A_HEADS, 1), F32),
                        pltpu.VMEM((2 * A_HEADS, A_HD2), F32)],
    )
    out = pl.pallas_call(
        functools.partial(_decode_kernel, lam_init=lam_init, past_len=past_len),
        grid_spec=grid_spec,
        out_shape=jax.ShapeDtypeStruct((db, A_HEADS, A_HD2), F32),
        compiler_params=_cparams(("parallel", "arbitrary")),
        name="decode_attn",
    )(page_table, lam_rows, hd(q_s.astype(F32)), hd(k_s), hd(v_s), jnp.asarray(bias), jnp.asarray(slope_col),
      *([cache_k] * DEC_PAGES), *([cache_v] * DEC_PAGES))
    return out.reshape(db, A_QK)


G_LEVELS = tuple(G_CHUNK >> (i + 1) for i in range(G_CHUNK.bit_length() - 1))


def _gla_decay_matrix():
    c = G_CHUNK
    r = np.arange(c)[:, None]
    j = np.arange(c)[None, :]
    blocks = [j <= r, j > r]
    for blk in G_LEVELS:
        ref = (r // (2 * blk)) * (2 * blk) + blk
        blocks.append(np.where(r >= ref, (j > ref) & (j <= r), (j > r) & (j <= ref)))
    d = np.concatenate(blocks, axis=0).astype(np.float32)
    return np.concatenate([d, d, d], axis=1)


def _gla_level_masks():
    c = G_CHUNK
    n = G_HEADS * c
    row = lax.broadcasted_iota(jnp.int32, (c, 1), 0)
    t_i = lax.broadcasted_iota(jnp.int32, (n, n), 0)
    s_i = lax.broadcasted_iota(jnp.int32, (n, n), 1)
    later, owns = [], []
    for blk in G_LEVELS:
        shift = blk.bit_length()
        later.append((row & blk) != 0)
        owns.append(((t_i >> shift) == (s_i >> shift)) & ((t_i & blk) != 0) & ((s_i & blk) == 0))
    return later, owns, t_i == s_i


def _heads_to_rows(x, width):
    return jnp.concatenate([x[:, h * width:(h + 1) * width] for h in range(G_HEADS)], axis=0)


def _gla_chunk(q, k, v, la, states, dmat, masks):
    c = q.shape[0]
    later, owns, diagonal = masks
    terms = jnp.concatenate(_bf16_terms(la), axis=0).astype(BF16)
    e = jnp.dot(dmat, terms, preferred_element_type=F32)
    b = e[0:c]
    vb = v.astype(BF16)

    qe = (q * jnp.exp2(b)).astype(BF16)
    qk = _heads_to_rows(q * k, G_DK)
    a = jnp.where(diagonal, jnp.sum(qk, axis=-1, keepdims=True), 0.0)
    for i in range(len(G_LEVELS)):
        x = (jnp.where(later[i], q, k) * jnp.exp2(e[(2 + i) * c:(3 + i) * c])).astype(BF16)
        xs = _heads_to_rows(x, G_DK)
        a = jnp.where(owns[i], lax.dot_general(xs, xs, NT_DIMS, preferred_element_type=F32), a)
    o_intra = jnp.dot(a.astype(BF16), _heads_to_rows(vb, G_DV), preferred_element_type=F32)

    k_dec = (k * jnp.exp2(e[c:2 * c])).astype(BF16)
    decay = jnp.exp2(b[c - 1:c])
    outs, new_states = [], []
    for h in range(G_HEADS):
        ks = slice(h * G_DK, (h + 1) * G_DK)
        st = states[h]
        o_inter = lax.dot_general(qe[:, ks], st.astype(BF16), NT_DIMS, preferred_element_type=F32)
        outs.append(o_inter + o_intra[h * c:(h + 1) * c])
        new_states.append(st * decay[:, ks] + lax.dot_general(
            vb[:, h * G_DV:(h + 1) * G_DV], k_dec[:, ks], TN_DIMS, preferred_element_type=F32))
    return outs, new_states


def _gla_kernel(dmat_ref, q_ref, k_ref, v_ref, la_ref, o_ref, s_ref, st_sc, *, valid_len):
    i = pl.program_id(1)

    @pl.when(i == 0)
    def _():
        st_sc[...] = jnp.zeros_like(st_sc)

    has_real_rows = i * GLA_ROWS < valid_len

    @pl.when(has_real_rows)
    def _():
        dmat = dmat_ref[...]
        masks = _gla_level_masks()
        states = [st_sc[h] for h in range(G_HEADS)]
        for c in range(GLA_ROWS // G_CHUNK):
            sl = slice(c * G_CHUNK, (c + 1) * G_CHUNK)
            pos = i * GLA_ROWS + c * G_CHUNK + lax.broadcasted_iota(jnp.int32, (G_CHUNK, 1), 0)
            valid = pos < valid_len
            la = jnp.where(valid, la_ref[sl, :], 0.0)
            k = jnp.where(valid, k_ref[sl, :], 0.0)
            outs, states = _gla_chunk(q_ref[sl, :], k, v_ref[sl, :], la, states, dmat, masks)
            for h in range(G_HEADS):
                o_ref[sl, h * G_DV:(h + 1) * G_DV] = outs[h]
        for h in range(G_HEADS):
            st_sc[h] = states[h]

    @pl.when(jnp.logical_not(has_real_rows))
    def _():
        o_ref[...] = jnp.zeros_like(o_ref)

    @pl.when(i == pl.num_programs(1) - 1)
    def _():
        for h in range(G_HEADS):
            s_ref[h] = st_sc[h].T


def gla_prompt(q, k, v, la, batch, lp, valid_len):
    nblk = lp // GLA_ROWS
    qk_spec = pl.BlockSpec((GLA_ROWS, G_QK), lambda b, i: (b * nblk + i, 0))
    v_spec = pl.BlockSpec((GLA_ROWS, G_V), lambda b, i: (b * nblk + i, 0))
    dmat = jnp.asarray(_gla_decay_matrix(), BF16)
    return pl.pallas_call(
        functools.partial(_gla_kernel, valid_len=valid_len),
        grid=(batch, nblk),
        in_specs=[pl.BlockSpec(dmat.shape, lambda b, i: (0, 0)), qk_spec, qk_spec, v_spec, qk_spec],
        out_specs=[v_spec, pl.BlockSpec((None, G_HEADS, G_DK, G_DV), lambda b, i: (b, 0, 0, 0))],
        out_shape=[jax.ShapeDtypeStruct((batch * lp, G_V), F32),
                   jax.ShapeDtypeStruct((batch, G_HEADS, G_DK, G_DV), F32)],
        scratch_shapes=[pltpu.VMEM((G_HEADS, G_DV, G_DK), F32)],
        compiler_params=_cparams(("parallel", "arbitrary")),
        name="gla_prompt",
    )(dmat, q, k, v, la)


def _stack_rows(rows, n_rows=16):
    n = rows[0].shape[1]
    idx = lax.broadcasted_iota(jnp.int32, (n_rows, n), 0)
    out = jnp.zeros((n_rows, n), F32)
    for r, x in enumerate(rows):
        out = jnp.where(idx == r, x, out)
    return out.astype(BF16)


def _gla_step_kernel(q_ref, k_ref, v_ref, la_ref, s_ref, o_ref, so_ref):
    q = q_ref[...]
    k = k_ref[...]
    v = v_ref[...]
    a = jnp.exp2(la_ref[...])
    ones = jnp.ones((16, G_DV), BF16)
    o_parts = []
    for h in range(G_HEADS):
        ks = slice(h * G_DK, (h + 1) * G_DK)
        vs = slice(h * G_DV, (h + 1) * G_DV)
        a_h = a[:, ks]
        a1 = a_h.astype(BF16)
        r1 = a_h - a1.astype(F32)
        a2 = r1.astype(BF16)
        a3 = (r1 - a2.astype(F32)).astype(BF16)
        a_rows = _stack_rows([a1.astype(F32), a2.astype(F32), a3.astype(F32)])
        a_col = lax.dot_general(a_rows, ones, TN_DIMS, preferred_element_type=F32)
        kv = lax.dot_general(_stack_rows([k[:, ks]]), _stack_rows([v[:, vs]]), TN_DIMS,
                             preferred_element_type=F32)
        s1 = a_col * s_ref[h] + kv
        so_ref[h] = s1
        o = jnp.dot(_stack_rows([q[:, ks]]), s1.astype(BF16), preferred_element_type=F32)
        o_parts.append(o[0:1])
    o_ref[...] = jnp.concatenate(o_parts, axis=1)


def gla_step(q_s, k_s, v_s, la_s, state_gla, layer):
    db = q_s.shape[0]
    r3 = lambda a: a.reshape(db, 1, a.shape[1])
    vec = lambda n: pl.BlockSpec((None, 1, n), lambda b: (b, 0, 0))
    o, s_new = pl.pallas_call(
        _gla_step_kernel,
        grid=(db,),
        in_specs=[vec(G_QK), vec(G_QK), vec(G_V), vec(G_QK),
                  pl.BlockSpec((None, None, G_HEADS, G_DK, G_DV), lambda b: (b, layer, 0, 0, 0))],
        out_specs=[vec(G_V), pl.BlockSpec((None, G_HEADS, G_DK, G_DV), lambda b: (b, 0, 0, 0))],
        out_shape=[jax.ShapeDtypeStruct((db, 1, G_V), F32),
                   jax.ShapeDtypeStruct((db, G_HEADS, G_DK, G_DV), F32)],
        compiler_params=_cparams(("parallel",)),
        name="gla_step",
    )(r3(q_s), r3(k_s), r3(v_s), r3(la_s), state_gla)
    return o.reshape(db, G_V), s_new


def kernel(x_prompt, x_sample, cache_k, cache_v, state_gla, page_table, meta_tokens, attn_w_in, attn_lq1, attn_lk1, attn_lq2, attn_lk2, attn_subln_w, attn_w_out, gla_w_in, gla_w_gate_up, gla_b_gate, gla_norm_w, gla_w_out, ln_g, ln_b):
    batch, seq, _ = x_prompt.shape
    db = x_sample.shape[0]
    length = N_META + seq
    lp = -(-length // ATT_Q) * ATT_Q
    meta = jnp.broadcast_to(meta_tokens[None].astype(x_prompt.dtype), (batch, N_META, D_MODEL))
    h_p = lax.pad(x_prompt, jnp.zeros((), x_prompt.dtype), ((0, 0, 0), (N_META, lp - length, 0), (0, 0, 0)))
    h_p = lax.dynamic_update_slice(h_p, meta, (0, 0, 0)).reshape(batch * lp, D_MODEL)
    h_s = x_sample.reshape(db, D_MODEL)

    assert DEPTH == 4
    kv_p = ()
    k_s_rows, v_s_rows, s_p_list, s_s_list = [], [], [], []
    for i in range(DEPTH):
        if i % 2 == 0:
            a = i // 2
            lam_init = 0.8 - 0.6 * math.exp(-0.3 * i)
            w_in = attn_w_in[a].astype(BF16)
            w_out = attn_w_out[a].astype(BF16)
            lam_rows = jnp.stack([attn_lq1[a], attn_lk1[a], attn_lq2[a], attn_lk2[a]]).astype(F32)
            q_p, k_p, v_p, g_p, ka_p, vt_p = attn_proj(h_p, w_in, ATT_TILE, batch, length, kv_p)
            kv_p = (k_p, v_p)
            q_s, k_s, v_s, g_s = attn_proj(h_s, w_in, db)
            o_p = flash_prompt(q_p, ka_p, vt_p, lam_rows, lam_init, batch, lp, length)
            o_s = decode_attn(q_s, k_s, v_s, cache_k, cache_v, page_table, a, lam_rows, lam_init)
            fin = functools.partial(finish, w_bf=w_out, norm_w=attn_subln_w[a], ln_g=ln_g[i], ln_b=ln_b[i],
                                    head_dim=A_HD2, scale=1.0 - lam_init)
            h_p = fin(o_p, g_p, h_p, tm=FINISH_ROWS)
            h_s = fin(o_s, g_s, h_s, tm=db)
            k_s_rows.append(k_s.reshape(db, 1, A_HEADS, A_HD2))
            v_s_rows.append(v_s.reshape(db, 1, A_HEADS, A_HD2))
        else:
            gi = i // 2
            extra = G_RANK_PAD - G_RANK
            n_main = 2 * G_QK + 2 * G_V
            w_in = gla_w_in[gi][:, :n_main].astype(BF16)
            w_rank = jnp.pad(gla_w_in[gi][:, n_main:], ((0, 0), (0, extra))).astype(BF16)
            w_up = jnp.pad(gla_w_gate_up[gi], ((0, extra), (0, 0))).astype(BF16)
            w_out = gla_w_out[gi].astype(BF16)
            b_gate = gla_b_gate[gi].reshape(1, G_QK).astype(F32)
            q_p, k_p, v_p, g_p, la_p = gla_proj(h_p, w_in, w_rank, w_up, b_gate, PROJ_ROWS)
            q_s, k_s, v_s, g_s, la_s = gla_proj(h_s, w_in, w_rank, w_up, b_gate, db)
            o_p, s_p = gla_prompt(q_p, k_p, v_p, la_p, batch, lp, length)
            o_s, s_s = gla_step(q_s, k_s, v_s, la_s, state_gla, gi)
            fin = functools.partial(finish, w_bf=w_out, norm_w=gla_norm_w[gi], ln_g=ln_g[i], ln_b=ln_b[i],
                                    head_dim=G_DV, scale=1.0)
            h_p = fin(o_p, g_p, h_p, tm=FINISH_ROWS)
            h_s = fin(o_s, g_s, h_s, tm=db)
            s_p_list.append(s_p)
            s_s_list.append(s_s)

    y_prompt = h_p.reshape(batch, lp, D_MODEL)[:, N_META:length]
    kv_shape = (batch, DEPTH // 2, length, A_HEADS, A_HD2)
    return (y_prompt, h_s.reshape(db, 1, D_MODEL),
            kv_p[0].reshape(kv_shape), kv_p[1].reshape(kv_shape), jnp.stack(s_p_list, axis=1),
            jnp.stack(k_s_rows, axis=1), jnp.stack(v_s_rows, axis=1), jnp.stack(s_s_list, axis=1))
```
